```python
import jax, jax.numpy as jnp
from jax import lax
import numpy as np

D_MODEL = 4096
BATCH = 1
SEQ = 16384
DEPTH = 4

N_META = 16
GRID_W = 64
ATT_HEADS = 16
ATT_HEAD_DIM = 128
ATT_WIDTH = ATT_HEADS * ATT_HEAD_DIM
WIN_ROWS_MAX = 8
WIN_COLS = 16
MLSTM_HEADS = 4
MLSTM_QK_DIM = 256
MLSTM_V_DIM = 512
MLSTM_QK_WIDTH = MLSTM_HEADS * MLSTM_QK_DIM
MLSTM_V_WIDTH = MLSTM_HEADS * MLSTM_V_DIM
MLSTM_CHUNK = 64
CONV_WIDTH = 5
N_GATES = 4 * MLSTM_HEADS
MIX_WIDTH = ATT_WIDTH + MLSTM_V_WIDTH
IN_WIDTH = 3 * ATT_WIDTH + 2 * MLSTM_QK_WIDTH + 2 * MLSTM_V_WIDTH + N_GATES
D_FF = -(-8 * D_MODEL // (3 * 256)) * 256
RMS_EPS = 1e-6

kernel_name = 'hymba_natten_mlstm_encoder'


def rms_norm(x, g):
    xf = x.astype(jnp.float32)
    y = xf * lax.rsqrt(jnp.mean(xf * xf, axis=-1, keepdims=True) + RMS_EPS)
    return (y * g.astype(jnp.float32)).astype(x.dtype)


def centred_conv(x, w, b):
    K = w.shape[0]
    half = K // 2
    T = x.shape[1]
    xp = jnp.pad(x, ((0, 0), (half, K - 1 - half), (0, 0)))
    y = xp[:, 0:T] * w[0]
    for j in range(1, K):
        y = y + xp[:, j:j + T] * w[j]
    return y + b


def neighbourhood_attention(q, k, v, rel_bias, meta_bias):
    B, T, H, d = q.shape
    n_tok = T - N_META
    rows = n_tok // GRID_W
    kr = min(WIN_ROWS_MAX, rows)
    scale = d ** -0.5
    f32 = jnp.float32
    qm, km, vm = q[:, :N_META], k[:, :N_META], v[:, :N_META]
    qg = q[:, N_META:].reshape(B, rows, GRID_W, H, d)
    kg = k[:, N_META:].reshape(B, rows, GRID_W, H, d)
    vg = v[:, N_META:].reshape(B, rows, GRID_W, H, d)
    cols = jnp.arange(GRID_W)
    row_start = jnp.clip(jnp.arange(rows) - kr // 2, 0, rows - kr)
    col_idx = (jnp.clip(cols - WIN_COLS // 2, 0, GRID_W - WIN_COLS)[:, None]
               + jnp.arange(WIN_COLS)[None, :])
    col_bias = rel_bias[:, :, col_idx - cols[:, None] + WIN_COLS - 1].astype(f32)
    meta_b = meta_bias.astype(f32)

    s_mm = jnp.einsum('bmhd,bnhd->bhmn', qm, km).astype(f32) * scale + meta_b[None, :, None, :]
    p_mm = jax.nn.softmax(s_mm, axis=-1).astype(v.dtype)
    o_meta = jnp.einsum('bhmn,bnhd->bmhd', p_mm, vm)

    def row_block(r):
        rs = row_start[r]
        q_r = qg[:, r]
        k_win = lax.dynamic_slice_in_dim(kg, rs, kr, axis=1)[:, :, col_idx]
        v_win = lax.dynamic_slice_in_dim(vg, rs, kr, axis=1)[:, :, col_idx]
        bias = col_bias[:, rs + jnp.arange(kr) - r + WIN_ROWS_MAX - 1]
        s_loc = (jnp.einsum('bchd,bicjhd->bhcij', q_r, k_win).astype(f32) * scale
                 + bias.transpose(0, 2, 1, 3)[None])
        s_met = jnp.einsum('bchd,bmhd->bhcm', q_r, km).astype(f32) * scale + meta_b[None, :, None, :]
        s = jnp.concatenate([s_loc.reshape(B, H, GRID_W, kr * WIN_COLS), s_met], axis=-1)
        p = jax.nn.softmax(s, axis=-1).astype(v.dtype)
        p_loc = p[..., :kr * WIN_COLS].reshape(B, H, GRID_W, kr, WIN_COLS)
        p_met = p[..., kr * WIN_COLS:]
        return (jnp.einsum('bhcij,bicjhd->bchd', p_loc, v_win)
                + jnp.einsum('bhcm,bmhd->bchd', p_met, vm))

    o_grid = lax.map(row_block, jnp.arange(rows))
    o_grid = o_grid.transpose(1, 0, 2, 3, 4).reshape(B, n_tok, H, d)
    return jnp.concatenate([o_meta, o_grid], axis=1)


def to_chunks(a, L):
    B, Tp = a.shape[:2]
    a = a.reshape((B, Tp // L, L) + a.shape[2:])
    return a.transpose((1, 0, 3, 2) + tuple(range(4, a.ndim)))


def from_chunks(a):
    NC, B, H, L, dv = a.shape
    return a.transpose(1, 0, 3, 2, 4).reshape(B, NC * L, H, dv)


def mlstm_chunk_scan(q, k, v, ig, lf):
    NC, B, H, L, dk = q.shape
    dv = v.shape[-1]
    causal = jnp.tril(jnp.ones((L, L), dtype=bool))

    def step(carry, xs):
        C, n, m = carry
        qc, kc, vc, ic, fc = xs
        b = jnp.cumsum(fc, axis=-1)
        b_tot = b[..., -1]
        log_d = jnp.where(causal, b[..., :, None] - b[..., None, :] + ic[..., None, :], -jnp.inf)
        log_inter = b + m[..., None]
        m_t = jnp.maximum(log_inter, jnp.max(log_d, axis=-1))
        d_mat = jnp.exp(log_d - m_t[..., None])
        inter = jnp.exp(log_inter - m_t)
        s = jnp.einsum('bhtk,bhsk->bhts', qc, kc) * d_mat
        num = (inter[..., None] * jnp.einsum('bhtk,bhkv->bhtv', qc, C)
               + jnp.einsum('bhts,bhsv->bhtv', s, vc))
        den = inter * jnp.einsum('bhtk,bhk->bht', qc, n) + jnp.sum(s, axis=-1)
        h = num / jnp.maximum(jnp.abs(den), jnp.exp(-m_t))[..., None]
        log_w = b_tot[..., None] - b + ic
        m_new = jnp.maximum(b_tot + m, jnp.max(log_w, axis=-1))
        decay = jnp.exp(b_tot + m - m_new)
        w = jnp.exp(log_w - m_new[..., None])
        C_new = decay[..., None, None] * C + jnp.einsum('bhs,bhsk,bhsv->bhkv', w, kc, vc)
        n_new = decay[..., None] * n + jnp.einsum('bhs,bhsk->bhk', w, kc)
        return (C_new, n_new, m_new), h

    f32 = jnp.float32
    init = (jnp.zeros((B, H, dk, dv), f32), jnp.zeros((B, H, dk), f32), jnp.zeros((B, H), f32))
    _, h = lax.scan(step, init, (q, k, v, ig, lf))
    return h


def bidirectional_mlstm(q, k, v, gate_pre, o_pre, norm_g):
    B, T, H, dk = q.shape
    dv = v.shape[-1]
    L = MLSTM_CHUNK
    pad = (-T) % L
    Tp = T + pad
    f32 = jnp.float32

    def pad_seq(a):
        return jnp.pad(a, ((0, 0), (pad, 0)) + ((0, 0),) * (a.ndim - 2))

    valid = (jnp.arange(Tp) >= pad)[None, :, None]
    qp = pad_seq(q.astype(f32) * dk ** -0.5)
    kp = pad_seq(k.astype(f32))
    vp = pad_seq(v.astype(f32))
    i_fw, f_fw, i_bw, f_bw = jnp.split(gate_pre.astype(f32), 4, axis=-1)

    def gates(i_pre, f_pre):
        ig = jnp.where(valid, pad_seq(i_pre), -jnp.inf)
        lf = jnp.where(valid, jax.nn.log_sigmoid(pad_seq(f_pre)), 0.0)
        return ig, lf

    def run(flip, i_pre, f_pre):
        ig, lf = gates(i_pre, f_pre)
        arrs = [qp, kp, vp, ig, lf]
        if flip:
            arrs = [jnp.flip(a, axis=1) for a in arrs]
        h = from_chunks(mlstm_chunk_scan(*[to_chunks(a, L) for a in arrs]))
        return jnp.flip(h, axis=1) if flip else h

    h = (run(False, i_fw, f_fw) + run(True, i_bw, f_bw))[:, pad:]
    h = h * lax.rsqrt(jnp.mean(h * h, axis=-1, keepdims=True) + RMS_EPS)
    h = h * norm_g.astype(f32).reshape(H, dv) * jax.nn.sigmoid(o_pre.astype(f32))
    return h.reshape(B, T, H * dv).astype(v.dtype)


def hybrid_layer(h, norm_mix_g, w_in, gate_bias, conv_w, conv_b, rel_bias, meta_bias,
                 mlstm_norm_g, w_out, norm_ffn_g, w_gate, w_up, w_down):
    B, T, D = h.shape
    u = rms_norm(h, norm_mix_g)
    proj = u @ w_in
    cuts = np.cumsum([ATT_WIDTH, ATT_WIDTH, ATT_WIDTH, MLSTM_QK_WIDTH, MLSTM_QK_WIDTH,
                      MLSTM_V_WIDTH, MLSTM_V_WIDTH]).tolist()
    aq, ak, av, mq, mk, mv, mo, gpre = jnp.split(proj, cuts, axis=-1)
    hs = (B, T, ATT_HEADS, ATT_HEAD_DIM)
    att = neighbourhood_attention(aq.reshape(hs), ak.reshape(hs), av.reshape(hs), rel_bias, meta_bias)
    att = att.reshape(B, T, ATT_WIDTH)
    qk = jax.nn.silu(centred_conv(jnp.concatenate([mq, mk], axis=-1), conv_w, conv_b))
    mq, mk = jnp.split(qk, 2, axis=-1)
    qs = (B, T, MLSTM_HEADS, MLSTM_QK_DIM)
    vs = (B, T, MLSTM_HEADS, MLSTM_V_DIM)
    mem = bidirectional_mlstm(mq.reshape(qs), mk.reshape(qs), mv.reshape(vs), gpre + gate_bias,
                              mo.reshape(vs), mlstm_norm_g)
    h = h + jnp.concatenate([att, mem], axis=-1) @ w_out
    z = rms_norm(h, norm_ffn_g)
    return h + (jax.nn.silu(z @ w_gate) * (z @ w_up)) @ w_down


def setup_inputs(seed: int = 0) -> dict:
    key = jax.random.key(seed)
    ks = jax.random.split(key, 20)
    f32 = jnp.float32

    def nrm(k, shape, s):
        return jax.random.normal(k, shape, f32) * s

    i_bias = nrm(ks[4], (DEPTH, 2, MLSTM_HEADS), 0.1)
    f_bias = 3.0 + 3.0 * jax.random.uniform(ks[5], (DEPTH, 2, MLSTM_HEADS), f32)
    gate_bias = jnp.stack([i_bias[:, 0], f_bias[:, 0], i_bias[:, 1], f_bias[:, 1]], axis=1).reshape(DEPTH, N_GATES)
    return {
        'x': nrm(ks[0], (BATCH, SEQ, D_MODEL), 1.0),
        'meta_tokens': nrm(ks[1], (N_META, D_MODEL), 1.0),
        'norm_mix_g': 1.0 + nrm(ks[2], (DEPTH, D_MODEL), 0.02),
        'w_in': nrm(ks[3], (DEPTH, D_MODEL, IN_WIDTH), D_MODEL ** -0.5),
        'gate_bias': gate_bias,
        'conv_w': nrm(ks[6], (DEPTH, CONV_WIDTH, 2 * MLSTM_QK_WIDTH), CONV_WIDTH ** -0.5),
        'conv_b': nrm(ks[7], (DEPTH, 2 * MLSTM_QK_WIDTH), 0.02),
        'rel_bias': nrm(ks[8], (DEPTH, ATT_HEADS, 2 * WIN_ROWS_MAX - 1, 2 * WIN_COLS - 1), 0.1),
        'meta_bias': nrm(ks[9], (DEPTH, ATT_HEADS, N_META), 0.1),
        'mlstm_norm_g': 1.0 + nrm(ks[10], (DEPTH, MLSTM_V_WIDTH), 0.02),
        'w_out': nrm(ks[11], (DEPTH, MIX_WIDTH, D_MODEL), MIX_WIDTH ** -0.5),
        'norm_ffn_g': 1.0 + nrm(ks[12], (DEPTH, D_MODEL), 0.02),
        'w_gate': nrm(ks[13], (DEPTH, D_MODEL, D_FF), D_MODEL ** -0.5),
        'w_up': nrm(ks[14], (DEPTH, D_MODEL, D_FF), D_MODEL ** -0.5),
        'w_down': nrm(ks[15], (DEPTH, D_FF, D_MODEL), D_FF ** -0.5),
        'final_norm_g': 1.0 + nrm(ks[16], (D_MODEL,), 0.02),
    }


def reference(x, meta_tokens, norm_mix_g, w_in, gate_bias, conv_w, conv_b, rel_bias, meta_bias,
              mlstm_norm_g, w_out, norm_ffn_g, w_gate, w_up, w_down, final_norm_g):
    B = x.shape[0]
    meta = jnp.broadcast_to(meta_tokens[None].astype(x.dtype), (B, N_META, x.shape[2]))
    h = jnp.concatenate([meta, x], axis=1)
    for l in range(DEPTH):
        h = hybrid_layer(h, norm_mix_g[l], w_in[l], gate_bias[l], conv_w[l], conv_b[l], rel_bias[l],
                         meta_bias[l], mlstm_norm_g[l], w_out[l], norm_ffn_g[l], w_gate[l], w_up[l], w_down[l])
    y = rms_norm(h, final_norm_g)
    return y[:, N_META:]
```

```python
import functools

import jax
import jax.numpy as jnp
from jax import lax
from jax.experimental import pallas as pl
from jax.experimental.pallas import tpu as pltpu

F32 = jnp.float32
BF16 = jnp.bfloat16

GRID_W = 64
WIN_ROWS = 8
WIN_COLS = 16
CHUNK = 64
RMS_EPS = 1e-6
NEG = -1e30
ROWS_PER_STEP = 8
HEADS_PER_STEP = 2

V7X_VMEM_LIMIT = 56 * 1024 * 1024


def _params(*sem):
    return pltpu.CompilerParams(dimension_semantics=sem, vmem_limit_bytes=V7X_VMEM_LIMIT)


def _divisor_tile(n, target, mult):
    best = None
    for d in range(mult, min(n, target) + 1, mult):
        if n % d == 0:
            best = d
    return best if best is not None else n


def _sigmoid(x):
    return 1.0 / (1.0 + jnp.exp(-x))


def _rmsnorm_kernel(x_ref, g_ref, o_ref):
    x = x_ref[...]
    ms = jnp.mean(x * x, axis=-1, keepdims=True)
    o_ref[...] = (x * lax.rsqrt(ms + RMS_EPS) * g_ref[...]).astype(o_ref.dtype)


def _rmsnorm(x, g, out_rows, out_dtype, tile):
    d = x.shape[1]
    return pl.pallas_call(
        _rmsnorm_kernel,
        grid=(out_rows // tile,),
        in_specs=[pl.BlockSpec((tile, d), lambda i: (i, 0)),
                  pl.BlockSpec((1, d), lambda i: (0, 0))],
        out_specs=pl.BlockSpec((tile, d), lambda i: (i, 0)),
        out_shape=jax.ShapeDtypeStruct((out_rows, d), out_dtype),
        name="rmsnorm",
        compiler_params=_params("parallel"),
    )(x, g.reshape(1, d))


def _mm_kernel(a_ref, w_ref, o_ref):
    o_ref[...] = jnp.dot(a_ref[...], w_ref[...], preferred_element_type=F32).astype(o_ref.dtype)


def _matmul(a, w, out_dtype, tm, tn):
    m, k = a.shape
    n = w.shape[1]
    tn = min(tn, n)
    return pl.pallas_call(
        _mm_kernel,
        grid=(m // tm, pl.cdiv(n, tn)),
        in_specs=[pl.BlockSpec((tm, k), lambda i, j: (i, 0)),
                  pl.BlockSpec((k, tn), lambda i, j: (0, j))],
        out_specs=pl.BlockSpec((tm, tn), lambda i, j: (i, j)),
        out_shape=jax.ShapeDtypeStruct((m, n), out_dtype),
        name="in_proj",
        compiler_params=_params("parallel", "arbitrary"),
    )(a, w)


def _mm2_res_kernel(a1_ref, a2_ref, w1_ref, w2_ref, r_ref, o_ref):
    acc = jnp.dot(a1_ref[...], w1_ref[...], preferred_element_type=F32)
    acc = acc + jnp.dot(a2_ref[...], w2_ref[...], preferred_element_type=F32)
    o_ref[...] = r_ref[...] + acc


def _matmul2_residual(a1, a2, w1, w2, res, tm, tn):
    m, k1 = a1.shape
    k2 = a2.shape[1]
    n = w1.shape[1]
    tn = min(tn, n)
    return pl.pallas_call(
        _mm2_res_kernel,
        grid=(m // tm, pl.cdiv(n, tn)),
        in_specs=[pl.BlockSpec((tm, k1), lambda i, j: (i, 0)),
                  pl.BlockSpec((tm, k2), lambda i, j: (i, 0)),
                  pl.BlockSpec((k1, tn), lambda i, j: (0, j)),
                  pl.BlockSpec((k2, tn), lambda i, j: (0, j)),
                  pl.BlockSpec((tm, tn), lambda i, j: (i, j))],
        out_specs=pl.BlockSpec((tm, tn), lambda i, j: (i, j)),
        out_shape=jax.ShapeDtypeStruct((m, n), F32),
        name="out_proj",
        compiler_params=_params("parallel", "arbitrary"),
    )(a1, a2, w1, w2, res)


def _mm_res_kernel(a_ref, w_ref, r_ref, o_ref):
    o_ref[...] = r_ref[...] + jnp.dot(a_ref[...], w_ref[...], preferred_element_type=F32)


def _matmul_residual(a, w, res, tm, tn):
    m, k = a.shape
    n = w.shape[1]
    tn = min(tn, n)
    return pl.pallas_call(
        _mm_res_kernel,
        grid=(m // tm, pl.cdiv(n, tn)),
        in_specs=[pl.BlockSpec((tm, k), lambda i, j: (i, 0)),
                  pl.BlockSpec((k, tn), lambda i, j: (0, j)),
                  pl.BlockSpec((tm, tn), lambda i, j: (i, j))],
        out_specs=pl.BlockSpec((tm, tn), lambda i, j: (i, j)),
        out_shape=jax.ShapeDtypeStruct((m, n), F32),
        name="down_proj",
        compiler_params=_params("parallel", "arbitrary"),
    )(a, w, res)


def _swiglu_kernel(a_ref, wg_ref, wu_ref, o_ref):
    a = a_ref[...]
    g = jnp.dot(a, wg_ref[...], preferred_element_type=F32)
    u = jnp.dot(a, wu_ref[...], preferred_element_type=F32)
    o_ref[...] = (g * _sigmoid(g) * u).astype(o_ref.dtype)


def _swiglu(a, wg, wu, tm, tn):
    m, k = a.shape
    n = wg.shape[1]
    tn = min(tn, n)
    return pl.pallas_call(
        _swiglu_kernel,
        grid=(m // tm, pl.cdiv(n, tn)),
        in_specs=[pl.BlockSpec((tm, k), lambda i, j: (i, 0)),
                  pl.BlockSpec((k, tn), lambda i, j: (0, j)),
                  pl.BlockSpec((k, tn), lambda i, j: (0, j))],
        out_specs=pl.BlockSpec((tm, tn), lambda i, j: (i, j)),
        out_shape=jax.ShapeDtypeStruct((m, n), BF16),
        name="swiglu",
        compiler_params=_params("parallel", "arbitrary"),
    )(a, wg, wu)


def _gates_kernel(u_ref, w_ref, wt_ref, b_ref, bt_ref, g_ref, gt_ref, *, n_meta, n_heads):
    u = u_ref[...]
    last = pl.program_id(0) == pl.num_programs(0) - 1

    def log_gates(pre, col_axis, row_axis):
        col = lax.broadcasted_iota(jnp.int32, pre.shape, col_axis)
        row = lax.broadcasted_iota(jnp.int32, pre.shape, row_axis)
        is_forget = (col // n_heads) % 2 == 1
        log_sig = jnp.minimum(pre, 0.0) - jnp.log(1.0 + jnp.exp(-jnp.abs(pre)))
        out = jnp.where(is_forget, log_sig, pre)
        pad = jnp.logical_and(last, row >= n_meta)
        return jnp.where(pad, jnp.where(is_forget, 0.0, NEG), out)

    pre = jnp.dot(u, w_ref[...], preferred_element_type=F32) + b_ref[...]
    g_ref[...] = log_gates(pre, 1, 0)
    pre_t = lax.dot_general(wt_ref[...], u, (((1,), (1,)), ((), ())),
                            preferred_element_type=F32) + bt_ref[...]
    gt_ref[0] = log_gates(pre_t, 0, 1)


def _gates(u, w, bias, n_chunks, n_meta, n_heads):
    d = u.shape[1]
    ng = w.shape[1]
    kern = functools.partial(_gates_kernel, n_meta=n_meta, n_heads=n_heads)
    return pl.pallas_call(
        kern,
        grid=(n_chunks,),
        in_specs=[pl.BlockSpec((CHUNK, d), lambda i: (i, 0)),
                  pl.BlockSpec((d, ng), lambda i: (0, 0)),
                  pl.BlockSpec((ng, d), lambda i: (0, 0)),
                  pl.BlockSpec((1, ng), lambda i: (0, 0)),
                  pl.BlockSpec((ng, 1), lambda i: (0, 0))],
        out_specs=[pl.BlockSpec((CHUNK, ng), lambda i: (i, 0)),
                   pl.BlockSpec((1, ng, CHUNK), lambda i: (i, 0, 0))],
        out_shape=[jax.ShapeDtypeStruct((n_chunks * CHUNK, ng), F32),
                   jax.ShapeDtypeStruct((n_chunks, ng, CHUNK), F32)],
        name="mlstm_gates",
        compiler_params=_params("parallel"),
    )(u, w, w.T, bias.reshape(1, ng), bias.reshape(ng, 1))


def _conv_kernel(x_ref, prev_ref, next_ref, w_ref, b_ref, s_ref, o_ref, ext_ref, *, n_meta, half):
    i = pl.program_id(0)
    n = pl.num_programs(0)
    tile = x_ref.shape[0]
    is_meta = i == n - 1
    row = lax.broadcasted_iota(jnp.int32, x_ref.shape, 0)
    x = jnp.where(jnp.logical_and(is_meta, row >= n_meta), 0.0, x_ref[...])
    ext_ref[0:8, :] = jnp.where(is_meta, 0.0, prev_ref[...])
    ext_ref[8:8 + tile, :] = x
    ext_ref[8 + tile:16 + tile, :] = jnp.where(i == n - 2, 0.0, next_ref[...])

    width = 2 * half + 1

    def conv_rows(start, rows):
        y = b_ref[...] + w_ref[0:1, :] * ext_ref[pl.ds(8 + start - half, rows), :]
        for j in range(1, width):
            y = y + w_ref[j:j + 1, :] * ext_ref[pl.ds(8 + start - half + j, rows), :]
        return y

    def finish(y):
        return (y * _sigmoid(y) * s_ref[...]).astype(o_ref.dtype)

    out = finish(conv_rows(0, tile))
    o_ref[...] = jnp.where(jnp.logical_and(is_meta, row >= n_meta), jnp.zeros_like(out), out)

    @pl.when(is_meta)
    def _():
        base = n_meta - 8
        y = conv_rows(base, 8)
        r8 = lax.broadcasted_iota(jnp.int32, y.shape, 0) + base
        for t in range(half):
            for j in range(half + 1 + t, width):
                src = j - half - 1 - t
                y = y + jnp.where(r8 == n_meta - 1 - t,
                                  w_ref[j:j + 1, :] * next_ref[src:src + 1, :], 0.0)
        o_ref[base:base + 8, :] = finish(y)


def _qk_conv(x, w, b, scale, n_tok, n_meta, tile):
    c = x.shape[1]
    n_grid_tiles = n_tok // tile
    kw = w.shape[0]
    half = kw // 2
    t8 = tile // 8
    last8 = (n_tok + n_meta) // 8 - 1

    def prev_map(i):
        return (jnp.where(i == 0, last8, jnp.minimum(i, n_grid_tiles) * t8 - 1), 0)

    def next_map(i):
        return (jnp.where(i >= n_grid_tiles - 1, 0, (i + 1) * t8), 0)

    kern = functools.partial(_conv_kernel, n_meta=n_meta, half=half)
    return pl.pallas_call(
        kern,
        grid=(n_grid_tiles + 1,),
        in_specs=[pl.BlockSpec((tile, c), lambda i: (i, 0)),
                  pl.BlockSpec((8, c), prev_map),
                  pl.BlockSpec((8, c), next_map),
                  pl.BlockSpec((kw, c), lambda i: (0, 0)),
                  pl.BlockSpec((1, c), lambda i: (0, 0)),
                  pl.BlockSpec((1, c), lambda i: (0, 0))],
        out_specs=pl.BlockSpec((tile, c), lambda i: (i, 0)),
        out_shape=jax.ShapeDtypeStruct((n_tok + CHUNK, c), BF16),
        scratch_shapes=[pltpu.VMEM((tile + 16, c), F32)],
        name="mlstm_qk_conv",
        compiler_params=_params("parallel"),
    )(x, x, x, w, b.reshape(1, c), scale.reshape(1, c))


def _scan_kernel(qkf_ref, qkb_ref, vf_ref, vb_ref, gf_ref, gb_ref, gtf_ref, gtb_ref,
                 hf_ref, hb_ref, c_ref, cb_ref, n_ref, m_ref, *, n_heads, dk, dv, n_meta):
    j = pl.program_id(0)
    nsteps = pl.num_programs(0)
    L = CHUNK

    @pl.when(j == 0)
    def _():
        c_ref[...] = jnp.zeros_like(c_ref)
        cb_ref[...] = jnp.zeros_like(cb_ref)
        n_ref[...] = jnp.zeros_like(n_ref)
        m_ref[...] = jnp.zeros_like(m_ref)

    t_idx = lax.broadcasted_iota(jnp.int32, (L, L), 0)
    s_idx = lax.broadcasted_iota(jnp.int32, (L, L), 1)
    lower = t_idx >= s_idx
    upper = t_idx <= s_idx
    row_l = lax.broadcasted_iota(jnp.int32, (L, 1), 0)

    dirs = (
        (qkf_ref, vf_ref, gf_ref, gtf_ref, hf_ref, lower, lower, upper, j == 0),
        (qkb_ref, vb_ref, gb_ref, gtb_ref, hb_ref, upper, upper, lower, j == nsteps - 1),
    )
    for d, (qk_ref, v_ref, g_ref, gt_ref, h_ref, mask, cum_col, cum_row, is_meta) in enumerate(dirs):
        g = g_ref[...]
        gt = gt_ref[0]
        c0 = 2 * d * n_heads
        ig_cols = g[:, c0:c0 + n_heads]
        lf_cols = g[:, c0 + n_heads:c0 + 2 * n_heads]
        ig_rows = gt[c0:c0 + n_heads, :]
        lf_rows = gt[c0 + n_heads:c0 + 2 * n_heads, :]
        b_cols = jnp.dot(cum_col.astype(F32), lf_cols, preferred_element_type=F32,
                         precision=lax.Precision.HIGHEST)
        b_rows = jnp.dot(lf_rows, cum_row.astype(F32), preferred_element_type=F32,
                         precision=lax.Precision.HIGHEST)
        b_tots = jnp.sum(lf_rows, axis=1, keepdims=True)
        valid = jnp.logical_or(jnp.logical_not(is_meta), row_l < n_meta)

        for hd in range(n_heads):
            ch = d * n_heads + hd
            q = qk_ref[:, hd * dk:(hd + 1) * dk]
            k = qk_ref[:, (n_heads + hd) * dk:(n_heads + hd + 1) * dk]
            v = v_ref[:, hd * dv:(hd + 1) * dv]
            v = jnp.where(valid, v, jnp.zeros_like(v))
            b_col = b_cols[:, hd:hd + 1]
            b_row = b_rows[hd:hd + 1, :]
            ig_col = ig_cols[:, hd:hd + 1]
            ig_row = ig_rows[hd:hd + 1, :]
            b_tot = b_tots[hd:hd + 1, :]
            m = m_ref[ch]

            log_d = jnp.where(mask, b_col - b_row + ig_row, NEG)
            log_inter = b_col + m
            m_t = jnp.maximum(log_inter, jnp.max(log_d, axis=1, keepdims=True))
            d_mat = jnp.exp(log_d - m_t)
            inter = jnp.exp(log_inter - m_t)
            s = lax.dot_general(q, k, (((1,), (1,)), ((), ())),
                                preferred_element_type=F32) * d_mat
            num = (inter * jnp.dot(q, cb_ref[ch], preferred_element_type=F32)
                   + jnp.dot(s.astype(BF16), v, preferred_element_type=F32))
            qn = jnp.sum(q.astype(F32) * n_ref[ch], axis=1, keepdims=True)
            den = inter * qn + jnp.sum(s, axis=1, keepdims=True)
            h = num / jnp.maximum(jnp.abs(den), jnp.exp(-m_t))
            h_ref[:, hd * dv:(hd + 1) * dv] = h

            log_w_row = b_tot - b_row + ig_row
            log_w_col = b_tot - b_col + ig_col
            m_new = jnp.maximum(b_tot + m, jnp.max(log_w_row, axis=1, keepdims=True))
            decay = jnp.exp(b_tot + m - m_new)
            w_col = jnp.exp(log_w_col - m_new)
            wv = (w_col * v.astype(F32)).astype(BF16)
            d_c = lax.dot_general(k, wv, (((0,), (0,)), ((), ())),
                                  preferred_element_type=F32)
            c_new = decay * c_ref[ch] + d_c
            c_ref[ch] = c_new
            cb_ref[ch] = c_new.astype(BF16)
            n_ref[ch] = decay * n_ref[ch] + jnp.sum(w_col * k.astype(F32), axis=0, keepdims=True)
            m_ref[ch] = m_new


def _mlstm_scan(qk, v, g, gt, n_tok, n_meta, n_heads, dk, dv):
    t_rows = v.shape[0]
    n_chunks = n_tok // CHUNK + 1

    def fwd(j):
        return (j + n_chunks - 1) % n_chunks

    def bwd(j):
        return (2 * (n_chunks - 1) - j) % n_chunks

    kern = functools.partial(_scan_kernel, n_heads=n_heads, dk=dk, dv=dv, n_meta=n_meta)
    ng = g.shape[1]
    return pl.pallas_call(
        kern,
        grid=(n_chunks,),
        in_specs=[pl.BlockSpec((CHUNK, qk.shape[1]), lambda j: (fwd(j), 0)),
                  pl.BlockSpec((CHUNK, qk.shape[1]), lambda j: (bwd(j), 0)),
                  pl.BlockSpec((CHUNK, v.shape[1]), lambda j: (fwd(j), 0)),
                  pl.BlockSpec((CHUNK, v.shape[1]), lambda j: (bwd(j), 0)),
                  pl.BlockSpec((CHUNK, ng), lambda j: (fwd(j), 0)),
                  pl.BlockSpec((CHUNK, ng), lambda j: (bwd(j), 0)),
                  pl.BlockSpec((1, ng, CHUNK), lambda j: (fwd(j), 0, 0)),
                  pl.BlockSpec((1, ng, CHUNK), lambda j: (bwd(j), 0, 0))],
        out_specs=[pl.BlockSpec((CHUNK, v.shape[1]), lambda j: (fwd(j), 0)),
                   pl.BlockSpec((CHUNK, v.shape[1]), lambda j: (bwd(j), 0))],
        out_shape=[jax.ShapeDtypeStruct((t_rows, v.shape[1]), F32),
                   jax.ShapeDtypeStruct((t_rows, v.shape[1]), F32)],
        scratch_shapes=[pltpu.VMEM((2 * n_heads, dk, dv), F32),
                        pltpu.VMEM((2 * n_heads, dk, dv), BF16),
                        pltpu.VMEM((2 * n_heads, 1, dk), F32),
                        pltpu.VMEM((2 * n_heads, 1, 1), F32)],
        name="mlstm_scan",
        compiler_params=_params("arbitrary"),
    )(qk, qk, v, v, g, g, gt, gt)


def _mlstm_out_kernel(hf_ref, hb_ref, o_ref, g_ref, out_ref, *, n_heads, dv):
    for hd in range(n_heads):
        sl = slice(hd * dv, (hd + 1) * dv)
        h = hf_ref[:, sl] + hb_ref[:, sl]
        ms = jnp.mean(h * h, axis=-1, keepdims=True)
        h = h * lax.rsqrt(ms + RMS_EPS)
        out_ref[:, sl] = (h * g_ref[:, sl] * _sigmoid(o_ref[:, sl])).astype(out_ref.dtype)


def _mlstm_out(hf, hb, o_pre, norm_g, n_heads, dv, tile):
    t_rows, width = hf.shape
    kern = functools.partial(_mlstm_out_kernel, n_heads=n_heads, dv=dv)
    spec = pl.BlockSpec((tile, width), lambda i: (i, 0))
    return pl.pallas_call(
        kern,
        grid=(t_rows // tile,),
        in_specs=[spec, spec, spec, pl.BlockSpec((1, width), lambda i: (0, 0))],
        out_specs=spec,
        out_shape=jax.ShapeDtypeStruct((t_rows, width), BF16),
        name="mlstm_out",
        compiler_params=_params("parallel"),
    )(hf, hb, o_pre, norm_g.reshape(1, width))


def _attn_kernel(q_ref, k_ref, v_ref, km_ref, vm_ref, bias_ref, mb_ref, o_ref, *, n_rows, hd_dim):
    blk = pl.program_id(1)
    scale = hd_dim ** -0.5
    nt = (((1,), (1,)), ((), ()))

    def row_body(rr, carry):
        r = blk * ROWS_PER_STEP + rr
        rs = jnp.clip(r - WIN_ROWS // 2, 0, n_rows - WIN_ROWS)
        variant = rs - r + WIN_ROWS - 1
        q0 = pl.multiple_of(rr * GRID_W, GRID_W)
        k0 = pl.multiple_of(rs * GRID_W, GRID_W)
        for hh in range(HEADS_PER_STEP):
            cols = slice(hh * hd_dim, (hh + 1) * hd_dim)
            q = q_ref[pl.ds(q0, GRID_W), cols]
            kw = k_ref[pl.ds(k0, WIN_ROWS * GRID_W), cols]
            vw = v_ref[pl.ds(k0, WIN_ROWS * GRID_W), cols]
            s = lax.dot_general(q, kw, nt, preferred_element_type=F32) * scale + bias_ref[variant, hh]
            sm = lax.dot_general(q, km_ref[:, cols], nt, preferred_element_type=F32) * scale + mb_ref[hh]
            mx = jnp.maximum(jnp.max(s, axis=1, keepdims=True), jnp.max(sm, axis=1, keepdims=True))
            p = jnp.exp(s - mx)
            pm = jnp.exp(sm - mx)
            denom = jnp.sum(p, axis=1, keepdims=True) + jnp.sum(pm, axis=1, keepdims=True)
            o = (jnp.dot(p.astype(BF16), vw, preferred_element_type=F32)
                 + jnp.dot(pm.astype(BF16), vm_ref[:, cols], preferred_element_type=F32))
            o_ref[pl.ds(q0, GRID_W), cols] = (o / denom).astype(o_ref.dtype)
        return carry

    lax.fori_loop(0, ROWS_PER_STEP, row_body, 0)


def _attn_meta_kernel(q_ref, k_ref, v_ref, mb_ref, att_in_ref, o_ref, *, n_heads, hd_dim):
    del att_in_ref
    scale = hd_dim ** -0.5
    for h in range(n_heads):
        cols = slice(h * hd_dim, (h + 1) * hd_dim)
        s = lax.dot_general(q_ref[:, cols], k_ref[:, cols], (((1,), (1,)), ((), ())),
                            preferred_element_type=F32) * scale + mb_ref[h]
        p = jnp.exp(s - jnp.max(s, axis=1, keepdims=True))
        denom = jnp.sum(p, axis=1, keepdims=True)
        o = jnp.dot(p.astype(BF16), v_ref[:, cols], preferred_element_type=F32)
        o_ref[:, cols] = (o / denom).astype(o_ref.dtype)


def _attention_bias(rel_bias):
    c = jnp.arange(GRID_W)
    cs = jnp.clip(c - WIN_COLS // 2, 0, GRID_W - WIN_COLS)
    in_win = (c[None, :] >= cs[:, None]) & (c[None, :] < cs[:, None] + WIN_COLS)
    dj = jnp.clip(c[None, :] - c[:, None] + WIN_COLS - 1, 0, 2 * WIN_COLS - 2)
    e = jnp.where(in_win[None, None], rel_bias[:, :, dj].astype(F32), NEG)
    di = jnp.arange(WIN_ROWS)[:, None] + jnp.arange(WIN_ROWS)[None, :]
    bv = e[:, di]
    bv = bv.transpose(1, 0, 3, 2, 4)
    return bv.reshape(WIN_ROWS, rel_bias.shape[0], GRID_W, WIN_ROWS * GRID_W)


def _attention(qkv, rel_bias, meta_bias, n_tok, n_meta, n_heads, hd_dim):
    t_rows = qkv.shape[0]
    n_rows = n_tok // GRID_W
    width = n_heads * hd_dim
    gw = HEADS_PER_STEP * hd_dim
    n_groups = n_heads // HEADS_PER_STEP
    q_tile = ROWS_PER_STEP * GRID_W
    bias = _attention_bias(rel_bias)
    mb = meta_bias.astype(F32).reshape(n_heads, 1, n_meta)
    meta_blk = n_tok // n_meta

    kern = functools.partial(_attn_kernel, n_rows=n_rows, hd_dim=hd_dim)
    att = pl.pallas_call(
        kern,
        grid=(n_groups, n_rows // ROWS_PER_STEP),
        in_specs=[pl.BlockSpec((q_tile, gw), lambda g, b: (b, g)),
                  pl.BlockSpec((n_tok, gw), lambda g, b: (0, n_groups + g)),
                  pl.BlockSpec((n_tok, gw), lambda g, b: (0, 2 * n_groups + g)),
                  pl.BlockSpec((n_meta, gw), lambda g, b: (meta_blk, n_groups + g)),
                  pl.BlockSpec((n_meta, gw), lambda g, b: (meta_blk, 2 * n_groups + g)),
                  pl.BlockSpec((WIN_ROWS, HEADS_PER_STEP, GRID_W, WIN_ROWS * GRID_W),
                               lambda g, b: (0, g, 0, 0)),
                  pl.BlockSpec((HEADS_PER_STEP, 1, n_meta), lambda g, b: (g, 0, 0))],
        out_specs=pl.BlockSpec((q_tile, gw), lambda g, b: (b, g)),
        out_shape=jax.ShapeDtypeStruct((t_rows, width), BF16),
        name="attn",
        compiler_params=_params("parallel", "arbitrary"),
    )(qkv, qkv, qkv, qkv, qkv, bias, mb)

    meta_kern = functools.partial(_attn_meta_kernel, n_heads=n_heads, hd_dim=hd_dim)
    return pl.pallas_call(
        meta_kern,
        grid=(1,),
        in_specs=[pl.BlockSpec((n_meta, width), lambda i: (meta_blk, 0)),
                  pl.BlockSpec((n_meta, width), lambda i: (meta_blk, 1)),
                  pl.BlockSpec((n_meta, width), lambda i: (meta_blk, 2)),
                  pl.BlockSpec((n_heads, 1, n_meta), lambda i: (0, 0, 0)),
                  pl.BlockSpec(memory_space=pl.ANY)],
        out_specs=pl.BlockSpec((n_meta, width), lambda i: (meta_blk, 0)),
        out_shape=jax.ShapeDtypeStruct((t_rows, width), BF16),
        input_output_aliases={4: 0},
        name="attn_meta",
        compiler_params=_params("arbitrary"),
    )(qkv, qkv, qkv, mb, att)


def kernel(x, meta_tokens, norm_mix_g, w_in, gate_bias, conv_w, conv_b, rel_bias, meta_bias,
           mlstm_norm_g, w_out, norm_ffn_g, w_gate, w_up, w_down, final_norm_g):
    batch, n_tok, d_model = x.shape
    assert batch == 1, "written for a single sequence"
    n_meta = meta_tokens.shape[0]
    depth = w_in.shape[0]
    t_rows = n_tok + n_meta
    att_heads = rel_bias.shape[1]
    n_gates = gate_bias.shape[1]
    ml_heads = n_gates // 4
    qk_width = conv_w.shape[2] // 2
    v_width = mlstm_norm_g.shape[1]
    att_width = (w_in.shape[2] - 2 * qk_width - 2 * v_width - n_gates) // 3
    hd_dim = att_width // att_heads
    dk = qk_width // ml_heads
    dv = v_width // ml_heads
    assert n_tok % (GRID_W * ROWS_PER_STEP) == 0 and n_meta % 16 == 0 and n_tok % n_meta == 0
    assert att_heads % HEADS_PER_STEP == 0

    tm = _divisor_tile(t_rows, 704, 16)
    tr = _divisor_tile(t_rows, 448, 16)
    conv_tile = _divisor_tile(n_tok, 512, CHUNK)
    n_chunks = n_tok // CHUNK + 1

    cuts = [0, 3 * att_width, 3 * att_width + 2 * qk_width, 3 * att_width + 2 * qk_width + v_width,
            3 * att_width + 2 * qk_width + 2 * v_width, w_in.shape[2]]
    q_scale = jnp.concatenate([jnp.full((qk_width,), dk ** -0.5, F32), jnp.ones((qk_width,), F32)])

    h = jnp.concatenate([x[0], meta_tokens.astype(x.dtype)], axis=0)
    for l in range(depth):
        w_in_l = w_in[l]
        w_seg = [w_in_l[:, cuts[s]:cuts[s + 1]].astype(BF16) for s in range(5)]
        u = _rmsnorm(h, norm_mix_g[l], t_rows, BF16, tr)
        qkv = _matmul(u, w_seg[0], BF16, tm, 1024)
        mqk = _matmul(u, w_seg[1], F32, tm, 1024)
        mv = _matmul(u, w_seg[2], BF16, tm, 1024)
        mo = _matmul(u, w_seg[3], F32, tm, 1024)
        g, gt = _gates(u, w_seg[4], gate_bias[l], n_chunks, n_meta, ml_heads)

        att = _attention(qkv, rel_bias[l], meta_bias[l], n_tok, n_meta, att_heads, hd_dim)

        qk = _qk_conv(mqk, conv_w[l], conv_b[l], q_scale, n_tok, n_meta, conv_tile)
        hf, hb = _mlstm_scan(qk, mv, g, gt, n_tok, n_meta, ml_heads, dk, dv)
        mem = _mlstm_out(hf, hb, mo, mlstm_norm_g[l], ml_heads, dv, tr)

        w_out_l = w_out[l]
        h = _matmul2_residual(att, mem, w_out_l[:att_width].astype(BF16),
                              w_out_l[att_width:].astype(BF16), h, tm, 1024)
        z = _rmsnorm(h, norm_ffn_g[l], t_rows, BF16, tr)
        act = _swiglu(z, w_gate[l].astype(BF16), w_up[l].astype(BF16), tm, 1024)
        h = _matmul_residual(act, w_down[l].astype(BF16), h, tm, 256)
    y = _rmsnorm(h, final_norm_g, n_tok, x.dtype, _divisor_tile(n_tok, 512, 16))
    return y[None]
```

```python
import functools

import jax
import jax.numpy as jnp
from jax import lax
from jax.experimental import pallas as pl
from jax.experimental.pallas import tpu as pltpu

F32 = jnp.float32
BF16 = jnp.bfloat16

GRID_W = 64
WIN_ROWS = 8
WIN_COLS = 16
CHUNK = 64
RMS_EPS = 1e-6
NEG = -1e30
ROWS_PER_STEP = 16
HEADS_PER_STEP = 2

V7X_VMEM_LIMIT = 56 * 1024 * 1024


def _params(*sem):
    return pltpu.CompilerParams(dimension_semantics=sem, vmem_limit_bytes=V7X_VMEM_LIMIT)


def _divisor_tile(n, target, mult):
    best = None
    for d in range(mult, min(n, target) + 1, mult):
        if n % d == 0:
            best = d
    return best if best is not None else n


def _sigmoid(x):
    return 1.0 / (1.0 + jnp.exp(-x))


def _rmsnorm_kernel(x_ref, g_ref, o_ref):
    x = x_ref[...]
    ms = jnp.mean(x * x, axis=-1, keepdims=True)
    o_ref[...] = (x * lax.rsqrt(ms + RMS_EPS) * g_ref[...]).astype(o_ref.dtype)


def _rmsnorm(x, g, out_rows, out_dtype, tile):
    d = x.shape[1]
    return pl.pallas_call(
        _rmsnorm_kernel,
        grid=(out_rows // tile,),
        in_specs=[pl.BlockSpec((tile, d), lambda i: (i, 0)),
                  pl.BlockSpec((1, d), lambda i: (0, 0))],
        out_specs=pl.BlockSpec((tile, d), lambda i: (i, 0)),
        out_shape=jax.ShapeDtypeStruct((out_rows, d), out_dtype),
        name="rmsnorm",
        compiler_params=_params("parallel"),
    )(x, g.reshape(1, d))


def _in_proj_kernel(u_ref, w_ref, wg_ref, bg_ref, *out_refs, bounds):
    j = pl.program_id(1)
    seg_refs, g_ref = out_refs[:-1], out_refs[-1]
    for s, o_ref in enumerate(seg_refs):
        @pl.when(jnp.logical_and(j >= bounds[s], j < bounds[s + 1]))
        def _(o_ref=o_ref):
            o_ref[...] = jnp.dot(u_ref[...], w_ref[...], preferred_element_type=F32).astype(o_ref.dtype)

    @pl.when(j == 0)
    def _():
        g_ref[...] = jnp.dot(u_ref[...], wg_ref[...], preferred_element_type=F32) + bg_ref[...]


def _in_proj(u, w, w_gates, gate_bias, seg_widths, seg_dtypes, tm):
    m, k = u.shape
    ng = w_gates.shape[1]
    tn = next(t for t in (1024, 512, 256, 128) if all(sw % t == 0 for sw in seg_widths))
    bounds = [0]
    for sw in seg_widths:
        bounds.append(bounds[-1] + sw // tn)

    def seg_map(s):
        return lambda i, j: (i, jnp.clip(j - bounds[s], 0, bounds[s + 1] - bounds[s] - 1))

    kern = functools.partial(_in_proj_kernel, bounds=tuple(bounds))
    return pl.pallas_call(
        kern,
        grid=(m // tm, bounds[-1]),
        in_specs=[pl.BlockSpec((tm, k), lambda i, j: (i, 0)),
                  pl.BlockSpec((k, tn), lambda i, j: (0, j)),
                  pl.BlockSpec((k, ng), lambda i, j: (0, 0)),
                  pl.BlockSpec((1, ng), lambda i, j: (0, 0))],
        out_specs=[pl.BlockSpec((tm, tn), seg_map(s)) for s in range(len(seg_widths))]
        + [pl.BlockSpec((tm, ng), lambda i, j: (i, 0))],
        out_shape=[jax.ShapeDtypeStruct((m, sw), dt) for sw, dt in zip(seg_widths, seg_dtypes)]
        + [jax.ShapeDtypeStruct((m, ng), F32)],
        name="in_proj",
        compiler_params=_params("parallel", "arbitrary"),
    )(u, w, w_gates, gate_bias.reshape(1, ng))


def _mm2_res_kernel(a1_ref, a2_ref, w1_ref, w2_ref, r_ref, o_ref):
    acc = jnp.dot(a1_ref[...], w1_ref[...], preferred_element_type=F32)
    acc = acc + jnp.dot(a2_ref[...], w2_ref[...], preferred_element_type=F32)
    o_ref[...] = r_ref[...] + acc


def _matmul2_residual(a1, a2, w1, w2, res, tm, tn):
    m, k1 = a1.shape
    k2 = a2.shape[1]
    n = w1.shape[1]
    tn = min(tn, n)
    return pl.pallas_call(
        _mm2_res_kernel,
        grid=(m // tm, pl.cdiv(n, tn)),
        in_specs=[pl.BlockSpec((tm, k1), lambda i, j: (i, 0)),
                  pl.BlockSpec((tm, k2), lambda i, j: (i, 0)),
                  pl.BlockSpec((k1, tn), lambda i, j: (0, j)),
                  pl.BlockSpec((k2, tn), lambda i, j: (0, j)),
                  pl.BlockSpec((tm, tn), lambda i, j: (i, j))],
        out_specs=pl.BlockSpec((tm, tn), lambda i, j: (i, j)),
        out_shape=jax.ShapeDtypeStruct((m, n), F32),
        name="out_proj",
        compiler_params=_params("parallel", "arbitrary"),
    )(a1, a2, w1, w2, res)


def _mm_res_kernel(a_ref, w_ref, r_ref, o_ref):
    o_ref[...] = r_ref[...] + jnp.dot(a_ref[...], w_ref[...], preferred_element_type=F32)


def _matmul_residual(a, w, res, tm, tn):
    m, k = a.shape
    n = w.shape[1]
    tn = min(tn, n)
    return pl.pallas_call(
        _mm_res_kernel,
        grid=(m // tm, pl.cdiv(n, tn)),
        in_specs=[pl.BlockSpec((tm, k), lambda i, j: (i, 0)),
                  pl.BlockSpec((k, tn), lambda i, j: (0, j)),
                  pl.BlockSpec((tm, tn), lambda i, j: (i, j))],
        out_specs=pl.BlockSpec((tm, tn), lambda i, j: (i, j)),
        out_shape=jax.ShapeDtypeStruct((m, n), F32),
        name="down_proj",
        compiler_params=_params("parallel", "arbitrary"),
    )(a, w, res)


def _swiglu_kernel(a_ref, wg_ref, wu_ref, o_ref):
    a = a_ref[...]
    g = jnp.dot(a, wg_ref[...].astype(a.dtype), preferred_element_type=F32)
    u = jnp.dot(a, wu_ref[...].astype(a.dtype), preferred_element_type=F32)
    o_ref[...] = (g * _sigmoid(g) * u).astype(o_ref.dtype)


def _swiglu(a, wg, wu, tm, tn):
    m, k = a.shape
    n = wg.shape[1]
    tn = min(tn, n)
    return pl.pallas_call(
        _swiglu_kernel,
        grid=(m // tm, pl.cdiv(n, tn)),
        in_specs=[pl.BlockSpec((tm, k), lambda i, j: (i, 0), pipeline_mode=pl.Buffered(1)),
                  pl.BlockSpec((k, tn), lambda i, j: (0, j)),
                  pl.BlockSpec((k, tn), lambda i, j: (0, j))],
        out_specs=pl.BlockSpec((tm, tn), lambda i, j: (i, j)),
        out_shape=jax.ShapeDtypeStruct((m, n), BF16),
        name="swiglu",
        compiler_params=_params("parallel", "arbitrary"),
    )(a, wg, wu)


def _conv_kernel(x_ref, prev_ref, next_ref, w_ref, b_ref, s_ref, o_ref, ext_ref, *, n_meta, half):
    i = pl.program_id(0)
    n = pl.num_programs(0)
    tile = x_ref.shape[0]
    is_meta = i == n - 1
    row = lax.broadcasted_iota(jnp.int32, x_ref.shape, 0)
    x = jnp.where(jnp.logical_and(is_meta, row >= n_meta), 0.0, x_ref[...])
    ext_ref[0:8, :] = jnp.where(is_meta, 0.0, prev_ref[...])
    ext_ref[8:8 + tile, :] = x
    ext_ref[8 + tile:16 + tile, :] = jnp.where(i == n - 2, 0.0, next_ref[...])

    width = 2 * half + 1

    def conv_rows(start, rows):
        y = b_ref[...] + w_ref[0:1, :] * ext_ref[pl.ds(8 + start - half, rows), :]
        for j in range(1, width):
            y = y + w_ref[j:j + 1, :] * ext_ref[pl.ds(8 + start - half + j, rows), :]
        return y

    def finish(y):
        return (y * _sigmoid(y) * s_ref[...]).astype(o_ref.dtype)

    out = finish(conv_rows(0, tile))
    o_ref[...] = jnp.where(jnp.logical_and(is_meta, row >= n_meta), jnp.zeros_like(out), out)

    @pl.when(is_meta)
    def _():
        base = n_meta - 8
        y = conv_rows(base, 8)
        r8 = lax.broadcasted_iota(jnp.int32, y.shape, 0) + base
        for t in range(half):
            for j in range(half + 1 + t, width):
                src = j - half - 1 - t
                y = y + jnp.where(r8 == n_meta - 1 - t,
                                  w_ref[j:j + 1, :] * next_ref[src:src + 1, :], 0.0)
        o_ref[base:base + 8, :] = finish(y)


def _qk_conv(x, w, b, scale, n_tok, n_meta, tile):
    c = x.shape[1]
    n_grid_tiles = n_tok // tile
    kw = w.shape[0]
    half = kw // 2
    t8 = tile // 8
    last8 = (n_tok + n_meta) // 8 - 1

    def prev_map(i):
        return (jnp.where(i == 0, last8, jnp.minimum(i, n_grid_tiles) * t8 - 1), 0)

    def next_map(i):
        return (jnp.where(i >= n_grid_tiles - 1, 0, (i + 1) * t8), 0)

    kern = functools.partial(_conv_kernel, n_meta=n_meta, half=half)
    return pl.pallas_call(
        kern,
        grid=(n_grid_tiles + 1,),
        in_specs=[pl.BlockSpec((tile, c), lambda i: (i, 0)),
                  pl.BlockSpec((8, c), prev_map),
                  pl.BlockSpec((8, c), next_map),
                  pl.BlockSpec((kw, c), lambda i: (0, 0)),
                  pl.BlockSpec((1, c), lambda i: (0, 0)),
                  pl.BlockSpec((1, c), lambda i: (0, 0))],
        out_specs=pl.BlockSpec((tile, c), lambda i: (i, 0)),
        out_shape=jax.ShapeDtypeStruct((n_tok + CHUNK, c), BF16),
        scratch_shapes=[pltpu.VMEM((tile + 16, c), F32)],
        name="mlstm_qk_conv",
        compiler_params=_params("parallel"),
    )(x, x, x, w, b.reshape(1, c), scale.reshape(1, c))


def _split_cumsum(mask, x, mask_first):
    m = mask.astype(BF16)
    total = None
    rest = x
    for _ in range(3):
        part = rest.astype(BF16)
        rest = rest - part.astype(F32)
        term = (jnp.dot(m, part, preferred_element_type=F32) if mask_first
                else jnp.dot(part, m, preferred_element_type=F32))
        total = term if total is None else total + term
    return total


def _scan_kernel(qkf_ref, qkb_ref, vf_ref, vb_ref, gf_ref, gb_ref,
                 hf_ref, hb_ref, c_ref, n_ref, m_ref, *, n_heads, dk, dv, n_meta):
    j = pl.program_id(0)
    nsteps = pl.num_programs(0)
    L = CHUNK

    @pl.when(j == 0)
    def _():
        c_ref[...] = jnp.zeros_like(c_ref)
        n_ref[...] = jnp.zeros_like(n_ref)
        m_ref[...] = jnp.zeros_like(m_ref)

    t_idx = lax.broadcasted_iota(jnp.int32, (L, L), 0)
    s_idx = lax.broadcasted_iota(jnp.int32, (L, L), 1)
    lower = t_idx >= s_idx
    upper = t_idx <= s_idx
    row_l = lax.broadcasted_iota(jnp.int32, (L, 1), 0)

    gate_col = lax.broadcasted_iota(jnp.int32, (L, 4 * n_heads), 1)
    is_forget = (gate_col // n_heads) % 2 == 1

    dirs = (
        (qkf_ref, vf_ref, gf_ref, hf_ref, lower, lower, upper, j == 0),
        (qkb_ref, vb_ref, gb_ref, hb_ref, upper, upper, lower, j == nsteps - 1),
    )
    chains = []
    for d, (qk_ref, v_ref, g_ref, h_ref, mask, cum_col, cum_row, is_meta) in enumerate(dirs):
        valid = jnp.logical_or(jnp.logical_not(is_meta), row_l < n_meta)
        pre = g_ref[...]
        log_sig = jnp.minimum(pre, 0.0) - jnp.log(1.0 + jnp.exp(-jnp.abs(pre)))
        g = jnp.where(valid, jnp.where(is_forget, log_sig, pre), jnp.where(is_forget, 0.0, NEG))
        gt = g.T
        c0 = 2 * d * n_heads
        ig_cols = g[:, c0:c0 + n_heads]
        lf_cols = g[:, c0 + n_heads:c0 + 2 * n_heads]
        ig_rows = gt[c0:c0 + n_heads, :]
        lf_rows = gt[c0 + n_heads:c0 + 2 * n_heads, :]
        b_cols = _split_cumsum(cum_col, lf_cols, True)
        b_rows = _split_cumsum(cum_row, lf_rows, False)
        b_tots = jnp.sum(lf_rows, axis=1, keepdims=True)

        for hd in range(n_heads):
            ch = d * n_heads + hd
            q = qk_ref[:, hd * dk:(hd + 1) * dk]
            k = qk_ref[:, (n_heads + hd) * dk:(n_heads + hd + 1) * dk]
            v = v_ref[:, hd * dv:(hd + 1) * dv]
            v = jnp.where(valid, v, jnp.zeros_like(v))
            b_col = b_cols[:, hd:hd + 1]
            b_row = b_rows[hd:hd + 1, :]
            ig_col = ig_cols[:, hd:hd + 1]
            ig_row = ig_rows[hd:hd + 1, :]
            b_tot = b_tots[hd:hd + 1, :]
            m = m_ref[ch]

            log_d = jnp.where(mask, b_col - b_row + ig_row, NEG)
            log_inter = b_col + m
            m_t = jnp.maximum(log_inter, jnp.max(log_d, axis=1, keepdims=True))
            log_w_row = b_tot - b_row + ig_row
            log_w_col = b_tot - b_col + ig_col
            m_new = jnp.maximum(b_tot + m, jnp.max(log_w_row, axis=1, keepdims=True))
            chains.append(dict(
                ch=ch, q=q, k=k, v=v, m_t=m_t, m_new=m_new, h_ref=h_ref,
                cols=slice(hd * dv, (hd + 1) * dv),
                d_mat=jnp.exp(log_d - m_t),
                inter=jnp.exp(log_inter - m_t),
                decay=jnp.exp(b_tot + m - m_new),
                w_col=jnp.exp(log_w_col - m_new)))

    for c in chains:
        c["c_old"] = c_ref[c["ch"]]
        c["qk"] = lax.dot_general(c["q"], c["k"], (((1,), (1,)), ((), ())),
                                  preferred_element_type=F32)
        c["qc"] = jnp.dot(c["q"], c["c_old"].astype(BF16), preferred_element_type=F32)
    for c in chains:
        c["s"] = c["qk"] * c["d_mat"]
        c["wv"] = (c["w_col"] * c["v"].astype(F32)).astype(BF16)
    for c in chains:
        c["sv"] = jnp.dot(c["s"].astype(BF16), c["v"], preferred_element_type=F32)
        c["d_c"] = lax.dot_general(c["k"], c["wv"], (((0,), (0,)), ((), ())),
                                   preferred_element_type=F32)
    for c in chains:
        ch, inter, decay, m_t = c["ch"], c["inter"], c["decay"], c["m_t"]
        num = inter * c["qc"] + c["sv"]
        qn = jnp.sum(c["q"].astype(F32) * n_ref[ch], axis=1, keepdims=True)
        den = inter * qn + jnp.sum(c["s"], axis=1, keepdims=True)
        c["h_ref"][:, c["cols"]] = num / jnp.maximum(jnp.abs(den), jnp.exp(-m_t))
        c_ref[ch] = decay * c["c_old"] + c["d_c"]
        n_ref[ch] = decay * n_ref[ch] + jnp.sum(c["w_col"] * c["k"].astype(F32), axis=0,
                                                keepdims=True)
        m_ref[ch] = c["m_new"]


def _mlstm_scan(qk, v, g, n_tok, n_meta, n_heads, dk, dv):
    t_rows = v.shape[0]
    n_chunks = n_tok // CHUNK + 1
    ng = g.shape[1]

    def fwd(j):
        return (j + n_chunks - 1) % n_chunks

    def bwd(j):
        return (2 * (n_chunks - 1) - j) % n_chunks

    kern = functools.partial(_scan_kernel, n_heads=n_heads, dk=dk, dv=dv, n_meta=n_meta)
    return pl.pallas_call(
        kern,
        grid=(n_chunks,),
        in_specs=[pl.BlockSpec((CHUNK, qk.shape[1]), lambda j: (fwd(j), 0)),
                  pl.BlockSpec((CHUNK, qk.shape[1]), lambda j: (bwd(j), 0)),
                  pl.BlockSpec((CHUNK, v.shape[1]), lambda j: (fwd(j), 0)),
                  pl.BlockSpec((CHUNK, v.shape[1]), lambda j: (bwd(j), 0)),
                  pl.BlockSpec((CHUNK, ng), lambda j: (fwd(j), 0)),
                  pl.BlockSpec((CHUNK, ng), lambda j: (bwd(j), 0))],
        out_specs=[pl.BlockSpec((CHUNK, v.shape[1]), lambda j: (fwd(j), 0)),
                   pl.BlockSpec((CHUNK, v.shape[1]), lambda j: (bwd(j), 0))],
        out_shape=[jax.ShapeDtypeStruct((t_rows, v.shape[1]), F32),
                   jax.ShapeDtypeStruct((t_rows, v.shape[1]), F32)],
        scratch_shapes=[pltpu.VMEM((2 * n_heads, dk, dv), F32),
                        pltpu.VMEM((2 * n_heads, 1, dk), F32),
                        pltpu.VMEM((2 * n_heads, 1, 1), F32)],
        name="mlstm_scan",
        compiler_params=_params("arbitrary"),
    )(qk, qk, v, v, g, g)


def _mlstm_out_kernel(hf_ref, hb_ref, o_ref, g_ref, out_ref, *, n_heads, dv):
    for hd in range(n_heads):
        sl = slice(hd * dv, (hd + 1) * dv)
        h = hf_ref[:, sl] + hb_ref[:, sl]
        ms = jnp.mean(h * h, axis=-1, keepdims=True)
        h = h * lax.rsqrt(ms + RMS_EPS)
        out_ref[:, sl] = (h * g_ref[:, sl] * _sigmoid(o_ref[:, sl])).astype(out_ref.dtype)


def _mlstm_out(hf, hb, o_pre, norm_g, n_heads, dv, tile):
    t_rows, width = hf.shape
    kern = functools.partial(_mlstm_out_kernel, n_heads=n_heads, dv=dv)
    spec = pl.BlockSpec((tile, width), lambda i: (i, 0))
    return pl.pallas_call(
        kern,
        grid=(t_rows // tile,),
        in_specs=[spec, spec, spec, pl.BlockSpec((1, width), lambda i: (0, 0))],
        out_specs=spec,
        out_shape=jax.ShapeDtypeStruct((t_rows, width), BF16),
        name="mlstm_out",
        compiler_params=_params("parallel"),
    )(hf, hb, o_pre, norm_g.reshape(1, width))


def _attn_kernel(q_ref, k_ref, v_ref, km_ref, vm_ref, bias_ref, mb_ref, o_ref, *, n_rows, hd_dim):
    blk = pl.program_id(1)
    scale = hd_dim ** -0.5
    nt = (((1,), (1,)), ((), ()))

    def meta_part(hh):
        cols = slice(hh * hd_dim, (hh + 1) * hd_dim)
        sm = lax.dot_general(q_ref[:, cols], km_ref[:, cols], nt,
                             preferred_element_type=F32) * scale + mb_ref[hh]
        m_meta = jnp.max(sm, axis=1, keepdims=True)
        pm = jnp.exp(sm - m_meta)
        l_meta = jnp.sum(pm, axis=1, keepdims=True)
        o_meta = jnp.dot(pm.astype(BF16), vm_ref[:, cols], preferred_element_type=F32)
        return m_meta, l_meta, o_meta

    units = [(rr, hh) for rr in range(ROWS_PER_STEP) for hh in range(HEADS_PER_STEP)]

    def window_start(rr):
        r = blk * ROWS_PER_STEP + rr
        rs = jnp.clip(r - WIN_ROWS // 2, 0, n_rows - WIN_ROWS)
        return pl.multiple_of(rs * GRID_W, GRID_W), rs - r + WIN_ROWS - 1

    def scores(rr, hh):
        cols = slice(hh * hd_dim, (hh + 1) * hd_dim)
        k0, variant = window_start(rr)
        q = q_ref[rr * GRID_W:(rr + 1) * GRID_W, cols]
        kw = k_ref[pl.ds(k0, WIN_ROWS * GRID_W), cols]
        return lax.dot_general(q, kw, nt, preferred_element_type=F32) * scale + bias_ref[variant, hh]

    def finish(rr, hh, s):
        cols = slice(hh * hd_dim, (hh + 1) * hd_dim)
        rows = slice(rr * GRID_W, (rr + 1) * GRID_W)
        k0, _ = window_start(rr)
        m_meta, l_meta, o_meta = meta[hh]
        m_loc = jnp.max(s, axis=1, keepdims=True)
        p = jnp.exp(s - m_loc)
        l_loc = jnp.sum(p, axis=1, keepdims=True)
        vw = v_ref[pl.ds(k0, WIN_ROWS * GRID_W), cols]
        o_loc = jnp.dot(p.astype(BF16), vw, preferred_element_type=F32)
        m_all = jnp.maximum(m_loc, m_meta[rows])
        a_loc = jnp.exp(m_loc - m_all)
        a_meta = jnp.exp(m_meta[rows] - m_all)
        o = (a_loc * o_loc + a_meta * o_meta[rows]) / (a_loc * l_loc + a_meta * l_meta[rows])
        o_ref[rows, cols] = o.astype(o_ref.dtype)

    ahead = 2
    pending = [scores(*units[u]) for u in range(ahead)]
    meta = [meta_part(hh) for hh in range(HEADS_PER_STEP)]
    for u, (rr, hh) in enumerate(units):
        if u + ahead < len(units):
            pending.append(scores(*units[u + ahead]))
        finish(rr, hh, pending.pop(0))


def _attn_meta_kernel(q_ref, k_ref, v_ref, mb_ref, att_in_ref, o_ref, *, n_heads, hd_dim):
    del att_in_ref
    scale = hd_dim ** -0.5
    for h in range(n_heads):
        cols = slice(h * hd_dim, (h + 1) * hd_dim)
        s = lax.dot_general(q_ref[:, cols], k_ref[:, cols], (((1,), (1,)), ((), ())),
                            preferred_element_type=F32) * scale + mb_ref[h]
        p = jnp.exp(s - jnp.max(s, axis=1, keepdims=True))
        denom = jnp.sum(p, axis=1, keepdims=True)
        o = jnp.dot(p.astype(BF16), v_ref[:, cols], preferred_element_type=F32)
        o_ref[:, cols] = (o / denom).astype(o_ref.dtype)


def _attention_bias(rel_bias):
    c = jnp.arange(GRID_W)
    cs = jnp.clip(c - WIN_COLS // 2, 0, GRID_W - WIN_COLS)
    in_win = (c[None, :] >= cs[:, None]) & (c[None, :] < cs[:, None] + WIN_COLS)
    dj = jnp.clip(c[None, :] - c[:, None] + WIN_COLS - 1, 0, 2 * WIN_COLS - 2)
    e = jnp.where(in_win[None, None], rel_bias[:, :, dj].astype(F32), NEG)
    di = jnp.arange(WIN_ROWS)[:, None] + jnp.arange(WIN_ROWS)[None, :]
    bv = e[:, di]
    bv = bv.transpose(1, 0, 3, 2, 4)
    return bv.reshape(WIN_ROWS, rel_bias.shape[0], GRID_W, WIN_ROWS * GRID_W)


def _attention(qkv, rel_bias, meta_bias, n_tok, n_meta, n_heads, hd_dim):
    t_rows = qkv.shape[0]
    n_rows = n_tok // GRID_W
    width = n_heads * hd_dim
    gw = HEADS_PER_STEP * hd_dim
    n_groups = n_heads // HEADS_PER_STEP
    q_tile = ROWS_PER_STEP * GRID_W
    bias = _attention_bias(rel_bias)
    mb = meta_bias.astype(F32).reshape(n_heads, 1, n_meta)
    meta_blk = n_tok // n_meta

    kern = functools.partial(_attn_kernel, n_rows=n_rows, hd_dim=hd_dim)
    att = pl.pallas_call(
        kern,
        grid=(n_groups, n_rows // ROWS_PER_STEP),
        in_specs=[pl.BlockSpec((q_tile, gw), lambda g, b: (b, g)),
                  pl.BlockSpec((n_tok, gw), lambda g, b: (0, n_groups + g)),
                  pl.BlockSpec((n_tok, gw), lambda g, b: (0, 2 * n_groups + g)),
                  pl.BlockSpec((n_meta, gw), lambda g, b: (meta_blk, n_groups + g)),
                  pl.BlockSpec((n_meta, gw), lambda g, b: (meta_blk, 2 * n_groups + g)),
                  pl.BlockSpec((WIN_ROWS, HEADS_PER_STEP, GRID_W, WIN_ROWS * GRID_W),
                               lambda g, b: (0, g, 0, 0)),
                  pl.BlockSpec((HEADS_PER_STEP, 1, n_meta), lambda g, b: (g, 0, 0))],
        out_specs=pl.BlockSpec((q_tile, gw), lambda g, b: (b, g)),
        out_shape=jax.ShapeDtypeStruct((t_rows, width), BF16),
        name="attn",
        compiler_params=_params("parallel", "arbitrary"),
    )(qkv, qkv, qkv, qkv, qkv, bias, mb)

    meta_kern = functools.partial(_attn_meta_kernel, n_heads=n_heads, hd_dim=hd_dim)
    return pl.pallas_call(
        meta_kern,
        grid=(1,),
        in_specs=[pl.BlockSpec((n_meta, width), lambda i: (meta_blk, 0)),
                  pl.BlockSpec((n_meta, width), lambda i: (meta_blk, 1)),
                  pl.BlockSpec((n_meta, width), lambda i: (meta_blk, 2)),
                  pl.BlockSpec((n_heads, 1, n_meta), lambda i: (0, 0, 0)),
                  pl.BlockSpec(memory_space=pl.ANY)],
        out_specs=pl.BlockSpec((n_meta, width), lambda i: (meta_blk, 0)),
        out_shape=jax.ShapeDtypeStruct((t_rows, width), BF16),
        input_output_aliases={4: 0},
        name="attn_meta",
        compiler_params=_params("arbitrary"),
    )(qkv, qkv, qkv, mb, att)


def kernel(x, meta_tokens, norm_mix_g, w_in, gate_bias, conv_w, conv_b, rel_bias, meta_bias,
           mlstm_norm_g, w_out, norm_ffn_g, w_gate, w_up, w_down, final_norm_g):
    batch, n_tok, d_model = x.shape
    assert batch == 1, "written for a single sequence"
    n_meta = meta_tokens.shape[0]
    depth = w_in.shape[0]
    t_rows = n_tok + n_meta
    att_heads = rel_bias.shape[1]
    n_gates = gate_bias.shape[1]
    ml_heads = n_gates // 4
    qk_width = conv_w.shape[2] // 2
    v_width = mlstm_norm_g.shape[1]
    att_width = (w_in.shape[2] - 2 * qk_width - 2 * v_width - n_gates) // 3
    hd_dim = att_width // att_heads
    dk = qk_width // ml_heads
    dv = v_width // ml_heads
    assert n_tok % (GRID_W * ROWS_PER_STEP) == 0 and n_meta % 16 == 0 and n_tok % n_meta == 0
    assert att_heads % HEADS_PER_STEP == 0

    tm = _divisor_tile(t_rows, 704, 16)
    tm_big = _divisor_tile(t_rows, 3328, 16)
    tr = _divisor_tile(t_rows, 448, 16)
    conv_tile = _divisor_tile(n_tok, 512, CHUNK)

    seg_widths = (3 * att_width, 2 * qk_width, v_width, v_width)
    seg_dtypes = (BF16, F32, BF16, F32)
    n_seg_cols = sum(seg_widths)
    q_scale = jnp.concatenate([jnp.full((qk_width,), dk ** -0.5, F32), jnp.ones((qk_width,), F32)])

    h = jnp.concatenate([x[0], meta_tokens.astype(x.dtype)], axis=0)
    for l in range(depth):
        w_in_l = w_in[l].astype(BF16)
        u = _rmsnorm(h, norm_mix_g[l], t_rows, BF16, tr)
        qkv, mqk, mv, mo, g = _in_proj(u, w_in_l, w_in_l[:, n_seg_cols:], gate_bias[l],
                                       seg_widths, seg_dtypes, tm)

        att = _attention(qkv, rel_bias[l], meta_bias[l], n_tok, n_meta, att_heads, hd_dim)

        qk = _qk_conv(mqk, conv_w[l], conv_b[l], q_scale, n_tok, n_meta, conv_tile)
        hf, hb = _mlstm_scan(qk, mv, g, n_tok, n_meta, ml_heads, dk, dv)
        mem = _mlstm_out(hf, hb, mo, mlstm_norm_g[l], ml_heads, dv, tr)

        w_out_l = w_out[l]
        h = _matmul2_residual(att, mem, w_out_l[:att_width].astype(BF16),
                              w_out_l[att_width:].astype(BF16), h, tm, 1024)
        z = _rmsnorm(h, norm_ffn_g[l], t_rows, BF16, tr)
        act = _swiglu(z, w_gate[l], w_up[l], tm_big, 256)
        h = _matmul_residual(act, w_down[l].astype(BF16), h, tm, 256)
    y = _rmsnorm(h, final_norm_g, n_tok, x.dtype, _divisor_tile(n_tok, 512, 16))
    return y[None]
```

```python
import functools

import jax
import jax.numpy as jnp
from jax import lax
from jax.experimental import pallas as pl
from jax.experimental.pallas import tpu as pltpu

F32 = jnp.float32
BF16 = jnp.bfloat16

GRID_W = 64
WIN_ROWS = 8
WIN_COLS = 16
CHUNK = 64
RMS_EPS = 1e-6
NEG = -1e30
ROWS_PER_STEP = 16
HEADS_PER_STEP = 2

V7X_VMEM_LIMIT = 56 * 1024 * 1024


def _params(*sem):
    return pltpu.CompilerParams(dimension_semantics=sem, vmem_limit_bytes=V7X_VMEM_LIMIT)


def _divisor_tile(n, target, mult):
    best = None
    for d in range(mult, min(n, target) + 1, mult):
        if n % d == 0:
            best = d
    return best if best is not None else n


def _sigmoid(x):
    return 1.0 / (1.0 + jnp.exp(-x))


def _rmsnorm_kernel(x_ref, g_ref, o_ref):
    x = x_ref[...]
    ms = jnp.mean(x * x, axis=-1, keepdims=True)
    o_ref[...] = (x * lax.rsqrt(ms + RMS_EPS) * g_ref[...]).astype(o_ref.dtype)


def _rmsnorm(x, g, out_rows, out_dtype, tile):
    d = x.shape[1]
    return pl.pallas_call(
        _rmsnorm_kernel,
        grid=(out_rows // tile,),
        in_specs=[pl.BlockSpec((tile, d), lambda i: (i, 0)),
                  pl.BlockSpec((1, d), lambda i: (0, 0))],
        out_specs=pl.BlockSpec((tile, d), lambda i: (i, 0)),
        out_shape=jax.ShapeDtypeStruct((out_rows, d), out_dtype),
        name="rmsnorm",
        compiler_params=_params("parallel"),
    )(x, g.reshape(1, d))


def _in_proj_kernel(u_ref, w_ref, wg_ref, bg_ref, *out_refs, bounds):
    j = pl.program_id(1)
    seg_refs, g_ref = out_refs[:-1], out_refs[-1]
    for s, o_ref in enumerate(seg_refs):
        @pl.when(jnp.logical_and(j >= bounds[s], j < bounds[s + 1]))
        def _(o_ref=o_ref):
            o_ref[...] = jnp.dot(u_ref[...], w_ref[...], preferred_element_type=F32).astype(o_ref.dtype)

    @pl.when(j == 0)
    def _():
        g_ref[...] = jnp.dot(u_ref[...], wg_ref[...], preferred_element_type=F32) + bg_ref[...]


def _in_proj(u, w, w_gates, gate_bias, layer, seg_widths, seg_dtypes, tm):
    m, k = u.shape
    ng = w_gates.shape[2]
    tn = next(t for t in (1024, 512, 256, 128) if all(sw % t == 0 for sw in seg_widths))
    bounds = [0]
    for sw in seg_widths:
        bounds.append(bounds[-1] + sw // tn)

    def seg_map(s):
        return lambda i, j: (i, jnp.clip(j - bounds[s], 0, bounds[s + 1] - bounds[s] - 1))

    kern = functools.partial(_in_proj_kernel, bounds=tuple(bounds))
    return pl.pallas_call(
        kern,
        grid=(m // tm, bounds[-1]),
        in_specs=[pl.BlockSpec((tm, k), lambda i, j: (i, 0)),
                  pl.BlockSpec((None, k, tn), lambda i, j: (layer, 0, j)),
                  pl.BlockSpec((None, k, ng), lambda i, j: (layer, 0, 0)),
                  pl.BlockSpec((1, ng), lambda i, j: (0, 0))],
        out_specs=[pl.BlockSpec((tm, tn), seg_map(s)) for s in range(len(seg_widths))]
        + [pl.BlockSpec((tm, ng), lambda i, j: (i, 0))],
        out_shape=[jax.ShapeDtypeStruct((m, sw), dt) for sw, dt in zip(seg_widths, seg_dtypes)]
        + [jax.ShapeDtypeStruct((m, ng), F32)],
        name="in_proj",
        compiler_params=_params("parallel", "arbitrary"),
    )(u, w, w_gates, gate_bias.reshape(1, ng))


def _mm_res_kernel(a_ref, w_ref, r_ref, o_ref):
    o_ref[...] = r_ref[...] + jnp.dot(a_ref[...], w_ref[...], preferred_element_type=F32)


def _matmul_residual(a, w, layer, res, tm, tn, name):
    m, k = a.shape
    n = w.shape[2]
    tn = min(tn, n)
    return pl.pallas_call(
        _mm_res_kernel,
        grid=(m // tm, pl.cdiv(n, tn)),
        in_specs=[pl.BlockSpec((tm, k), lambda i, j: (i, 0)),
                  pl.BlockSpec((None, k, tn), lambda i, j: (layer, 0, j)),
                  pl.BlockSpec((tm, tn), lambda i, j: (i, j))],
        out_specs=pl.BlockSpec((tm, tn), lambda i, j: (i, j)),
        out_shape=jax.ShapeDtypeStruct((m, n), F32),
        name=name,
        compiler_params=_params("parallel", "arbitrary"),
    )(a, w, res)


def _swiglu_kernel(a_ref, wg_ref, wu_ref, o_ref):
    a = a_ref[...]
    g = jnp.dot(a, wg_ref[...].astype(a.dtype), preferred_element_type=F32)
    u = jnp.dot(a, wu_ref[...].astype(a.dtype), preferred_element_type=F32)
    o_ref[...] = (g * _sigmoid(g) * u).astype(o_ref.dtype)


def _swiglu(a, wg, wu, layer, tm, tn):
    m, k = a.shape
    n = wg.shape[2]
    tn = min(tn, n)
    return pl.pallas_call(
        _swiglu_kernel,
        grid=(m // tm, pl.cdiv(n, tn)),
        in_specs=[pl.BlockSpec((tm, k), lambda i, j: (i, 0), pipeline_mode=pl.Buffered(1)),
                  pl.BlockSpec((None, k, tn), lambda i, j: (layer, 0, j)),
                  pl.BlockSpec((None, k, tn), lambda i, j: (layer, 0, j))],
        out_specs=pl.BlockSpec((tm, tn), lambda i, j: (i, j)),
        out_shape=jax.ShapeDtypeStruct((m, n), BF16),
        name="swiglu",
        compiler_params=_params("parallel", "arbitrary"),
    )(a, wg, wu)


def _conv_kernel(x_ref, prev_ref, next_ref, w_ref, b_ref, s_ref, o_ref, ext_ref, *, n_meta, half):
    i = pl.program_id(0)
    n = pl.num_programs(0)
    tile = x_ref.shape[0]
    is_meta = i == n - 1
    row = lax.broadcasted_iota(jnp.int32, x_ref.shape, 0)
    x = jnp.where(jnp.logical_and(is_meta, row >= n_meta), 0.0, x_ref[...])
    ext_ref[0:8, :] = jnp.where(is_meta, 0.0, prev_ref[...])
    ext_ref[8:8 + tile, :] = x
    ext_ref[8 + tile:16 + tile, :] = jnp.where(i == n - 2, 0.0, next_ref[...])

    width = 2 * half + 1

    def conv_rows(start, rows):
        y = b_ref[...] + w_ref[0:1, :] * ext_ref[pl.ds(8 + start - half, rows), :]
        for j in range(1, width):
            y = y + w_ref[j:j + 1, :] * ext_ref[pl.ds(8 + start - half + j, rows), :]
        return y

    def finish(y):
        return (y * _sigmoid(y) * s_ref[...]).astype(o_ref.dtype)

    out = finish(conv_rows(0, tile))
    o_ref[...] = jnp.where(jnp.logical_and(is_meta, row >= n_meta), jnp.zeros_like(out), out)

    @pl.when(is_meta)
    def _():
        base = n_meta - 8
        y = conv_rows(base, 8)
        r8 = lax.broadcasted_iota(jnp.int32, y.shape, 0) + base
        for t in range(half):
            for j in range(half + 1 + t, width):
                src = j - half - 1 - t
                y = y + jnp.where(r8 == n_meta - 1 - t,
                                  w_ref[j:j + 1, :] * next_ref[src:src + 1, :], 0.0)
        o_ref[base:base + 8, :] = finish(y)


def _qk_conv(x, w, b, scale, n_tok, n_meta, tile):
    c = x.shape[1]
    n_grid_tiles = n_tok // tile
    kw = w.shape[0]
    half = kw // 2
    t8 = tile // 8
    last8 = (n_tok + n_meta) // 8 - 1

    def prev_map(i):
        return (jnp.where(i == 0, last8, jnp.minimum(i, n_grid_tiles) * t8 - 1), 0)

    def next_map(i):
        return (jnp.where(i >= n_grid_tiles - 1, 0, (i + 1) * t8), 0)

    kern = functools.partial(_conv_kernel, n_meta=n_meta, half=half)
    return pl.pallas_call(
        kern,
        grid=(n_grid_tiles + 1,),
        in_specs=[pl.BlockSpec((tile, c), lambda i: (i, 0)),
                  pl.BlockSpec((8, c), prev_map),
                  pl.BlockSpec((8, c), next_map),
                  pl.BlockSpec((kw, c), lambda i: (0, 0)),
                  pl.BlockSpec((1, c), lambda i: (0, 0)),
                  pl.BlockSpec((1, c), lambda i: (0, 0))],
        out_specs=pl.BlockSpec((tile, c), lambda i: (i, 0)),
        out_shape=jax.ShapeDtypeStruct((n_tok + CHUNK, c), BF16),
        scratch_shapes=[pltpu.VMEM((tile + 16, c), F32)],
        name="mlstm_qk_conv",
        compiler_params=_params("parallel"),
    )(x, x, x, w, b.reshape(1, c), scale.reshape(1, c))


def _split_cumsum(mask, x, mask_first):
    m = mask.astype(BF16)
    total = None
    rest = x
    for _ in range(3):
        part = rest.astype(BF16)
        rest = rest - part.astype(F32)
        term = (jnp.dot(m, part, preferred_element_type=F32) if mask_first
                else jnp.dot(part, m, preferred_element_type=F32))
        total = term if total is None else total + term
    return total


def _scan_kernel(qkf_ref, qkb_ref, vf_ref, vb_ref, gf_ref, gb_ref,
                 hf_ref, hb_ref, c_ref, n_ref, m_ref, *, n_heads, dk, dv, n_meta):
    j = pl.program_id(0)
    nsteps = pl.num_programs(0)
    L = CHUNK

    @pl.when(j == 0)
    def _():
        c_ref[...] = jnp.zeros_like(c_ref)
        n_ref[...] = jnp.zeros_like(n_ref)
        m_ref[...] = jnp.zeros_like(m_ref)

    t_idx = lax.broadcasted_iota(jnp.int32, (L, L), 0)
    s_idx = lax.broadcasted_iota(jnp.int32, (L, L), 1)
    lower = t_idx >= s_idx
    upper = t_idx <= s_idx
    row_l = lax.broadcasted_iota(jnp.int32, (L, 1), 0)

    gate_col = lax.broadcasted_iota(jnp.int32, (L, 4 * n_heads), 1)
    is_forget = (gate_col // n_heads) % 2 == 1

    dirs = (
        (qkf_ref, vf_ref, gf_ref, hf_ref, lower, lower, upper, j == 0),
        (qkb_ref, vb_ref, gb_ref, hb_ref, upper, upper, lower, j == nsteps - 1),
    )
    chains = []
    for d, (qk_ref, v_ref, g_ref, h_ref, mask, cum_col, cum_row, is_meta) in enumerate(dirs):
        valid = jnp.logical_or(jnp.logical_not(is_meta), row_l < n_meta)
        pre = g_ref[...]
        log_sig = jnp.minimum(pre, 0.0) - jnp.log(1.0 + jnp.exp(-jnp.abs(pre)))
        g = jnp.where(valid, jnp.where(is_forget, log_sig, pre), jnp.where(is_forget, 0.0, NEG))
        gt = g.T
        c0 = 2 * d * n_heads
        ig_cols = g[:, c0:c0 + n_heads]
        lf_cols = g[:, c0 + n_heads:c0 + 2 * n_heads]
        ig_rows = gt[c0:c0 + n_heads, :]
        lf_rows = gt[c0 + n_heads:c0 + 2 * n_heads, :]
        b_cols = _split_cumsum(cum_col, lf_cols, True)
        b_rows = _split_cumsum(cum_row, lf_rows, False)
        b_tots = jnp.sum(lf_rows, axis=1, keepdims=True)

        for hd in range(n_heads):
            ch = d * n_heads + hd
            q = qk_ref[:, hd * dk:(hd + 1) * dk]
            k = qk_ref[:, (n_heads + hd) * dk:(n_heads + hd + 1) * dk]
            v = v_ref[:, hd * dv:(hd + 1) * dv]
            v = jnp.where(valid, v, jnp.zeros_like(v))
            b_col = b_cols[:, hd:hd + 1]
            b_row = b_rows[hd:hd + 1, :]
            ig_col = ig_cols[:, hd:hd + 1]
            ig_row = ig_rows[hd:hd + 1, :]
            b_tot = b_tots[hd:hd + 1, :]
            m = m_ref[ch]

            log_d = jnp.where(mask, b_col - b_row + ig_row, NEG)
            log_inter = b_col + m
            m_t = jnp.maximum(log_inter, jnp.max(log_d, axis=1, keepdims=True))
            log_w_row = b_tot - b_row + ig_row
            log_w_col = b_tot - b_col + ig_col
            m_new = jnp.maximum(b_tot + m, jnp.max(log_w_row, axis=1, keepdims=True))
            chains.append(dict(
                ch=ch, q=q, k=k, v=v, m_t=m_t, m_new=m_new, h_ref=h_ref,
                cols=slice(hd * dv, (hd + 1) * dv),
                d_mat=jnp.exp(log_d - m_t),
                inter=jnp.exp(log_inter - m_t),
                decay=jnp.exp(b_tot + m - m_new),
                w_col=jnp.exp(log_w_col - m_new)))

    for c in chains:
        c["qk"] = lax.dot_general(c["q"], c["k"], (((1,), (1,)), ((), ())),
                                  preferred_element_type=F32)
        c["qc"] = jnp.dot(c["q"], c_ref[c["ch"]].astype(BF16), preferred_element_type=F32)
    for c in chains:
        c["s"] = c["qk"] * c["d_mat"]
        c["wv"] = (c["w_col"] * c["v"].astype(F32)).astype(BF16)
        qn = jnp.sum(c["q"].astype(F32) * n_ref[c["ch"]], axis=1, keepdims=True)
        den = c["inter"] * qn + jnp.sum(c["s"], axis=1, keepdims=True)
        c["scale"] = 1.0 / jnp.maximum(jnp.abs(den), jnp.exp(-c["m_t"]))
    for c in chains:
        ch, decay = c["ch"], c["decay"]
        sv = jnp.dot(c["s"].astype(BF16), c["v"], preferred_element_type=F32)
        c["h_ref"][:, c["cols"]] = (c["inter"] * c["qc"] + sv) * c["scale"]
        d_c = lax.dot_general(c["k"], c["wv"], (((0,), (0,)), ((), ())),
                              preferred_element_type=F32)
        c_ref[ch] = decay * c_ref[ch] + d_c
        n_ref[ch] = decay * n_ref[ch] + jnp.sum(c["w_col"] * c["k"].astype(F32), axis=0,
                                                keepdims=True)
        m_ref[ch] = c["m_new"]


def _mlstm_scan(qk, v, g, n_tok, n_meta, n_heads, dk, dv):
    t_rows = v.shape[0]
    n_chunks = n_tok // CHUNK + 1
    ng = g.shape[1]

    def fwd(j):
        return (j + n_chunks - 1) % n_chunks

    def bwd(j):
        return (2 * (n_chunks - 1) - j) % n_chunks

    kern = functools.partial(_scan_kernel, n_heads=n_heads, dk=dk, dv=dv, n_meta=n_meta)
    return pl.pallas_call(
        kern,
        grid=(n_chunks,),
        in_specs=[pl.BlockSpec((CHUNK, qk.shape[1]), lambda j: (fwd(j), 0)),
                  pl.BlockSpec((CHUNK, qk.shape[1]), lambda j: (bwd(j), 0)),
                  pl.BlockSpec((CHUNK, v.shape[1]), lambda j: (fwd(j), 0)),
                  pl.BlockSpec((CHUNK, v.shape[1]), lambda j: (bwd(j), 0)),
                  pl.BlockSpec((CHUNK, ng), lambda j: (fwd(j), 0)),
                  pl.BlockSpec((CHUNK, ng), lambda j: (bwd(j), 0))],
        out_specs=[pl.BlockSpec((CHUNK, v.shape[1]), lambda j: (fwd(j), 0)),
                   pl.BlockSpec((CHUNK, v.shape[1]), lambda j: (bwd(j), 0))],
        out_shape=[jax.ShapeDtypeStruct((t_rows, v.shape[1]), F32),
                   jax.ShapeDtypeStruct((t_rows, v.shape[1]), F32)],
        scratch_shapes=[pltpu.VMEM((2 * n_heads, dk, dv), F32),
                        pltpu.VMEM((2 * n_heads, 1, dk), F32),
                        pltpu.VMEM((2 * n_heads, 1, 1), F32)],
        name="mlstm_scan",
        compiler_params=_params("arbitrary"),
    )(qk, qk, v, v, g, g)


def _mlstm_out_kernel(hf_ref, hb_ref, o_ref, g_ref, mix_in_ref, out_ref, *, n_heads, dv):
    del mix_in_ref
    for hd in range(n_heads):
        sl = slice(hd * dv, (hd + 1) * dv)
        h = hf_ref[:, sl] + hb_ref[:, sl]
        ms = jnp.mean(h * h, axis=-1, keepdims=True)
        h = h * lax.rsqrt(ms + RMS_EPS)
        out_ref[:, sl] = (h * g_ref[:, sl] * _sigmoid(o_ref[:, sl])).astype(out_ref.dtype)


def _mlstm_out(hf, hb, o_pre, norm_g, mix, n_heads, dv, tile):
    t_rows, width = hf.shape
    col_blk = (mix.shape[1] - width) // width
    assert col_blk * width == mix.shape[1] - width
    kern = functools.partial(_mlstm_out_kernel, n_heads=n_heads, dv=dv)
    spec = pl.BlockSpec((tile, width), lambda i: (i, 0))
    return pl.pallas_call(
        kern,
        grid=(t_rows // tile,),
        in_specs=[spec, spec, spec, pl.BlockSpec((1, width), lambda i: (0, 0)),
                  pl.BlockSpec(memory_space=pl.ANY)],
        out_specs=pl.BlockSpec((tile, width), lambda i: (i, col_blk)),
        out_shape=jax.ShapeDtypeStruct(mix.shape, mix.dtype),
        input_output_aliases={4: 0},
        name="mlstm_out",
        compiler_params=_params("parallel"),
    )(hf, hb, o_pre, norm_g.reshape(1, width), mix)


def _attn_kernel(q_ref, k_ref, v_ref, km_ref, vm_ref, bias_ref, mb_ref, o_ref, *, n_rows, hd_dim):
    blk = pl.program_id(1)
    scale = hd_dim ** -0.5
    nt = (((1,), (1,)), ((), ()))

    def meta_part(hh):
        cols = slice(hh * hd_dim, (hh + 1) * hd_dim)
        sm = lax.dot_general(q_ref[:, cols], km_ref[:, cols], nt,
                             preferred_element_type=F32) * scale + mb_ref[hh]
        m_meta = jnp.max(sm, axis=1, keepdims=True)
        pm = jnp.exp(sm - m_meta)
        l_meta = jnp.sum(pm, axis=1, keepdims=True)
        o_meta = jnp.dot(pm.astype(BF16), vm_ref[:, cols], preferred_element_type=F32)
        return m_meta, l_meta, o_meta

    units = [(rr, hh) for rr in range(ROWS_PER_STEP) for hh in range(HEADS_PER_STEP)]

    def window_start(rr):
        r = blk * ROWS_PER_STEP + rr
        rs = jnp.clip(r - WIN_ROWS // 2, 0, n_rows - WIN_ROWS)
        return pl.multiple_of(rs * GRID_W, GRID_W), rs - r + WIN_ROWS - 1

    def scores(rr, hh):
        cols = slice(hh * hd_dim, (hh + 1) * hd_dim)
        k0, variant = window_start(rr)
        q = q_ref[rr * GRID_W:(rr + 1) * GRID_W, cols]
        kw = k_ref[pl.ds(k0, WIN_ROWS * GRID_W), cols]
        return lax.dot_general(q, kw, nt, preferred_element_type=F32) * scale + bias_ref[variant, hh]

    def finish(rr, hh, s):
        cols = slice(hh * hd_dim, (hh + 1) * hd_dim)
        rows = slice(rr * GRID_W, (rr + 1) * GRID_W)
        k0, _ = window_start(rr)
        m_meta, l_meta, o_meta = meta[hh]
        m_loc = jnp.max(s, axis=1, keepdims=True)
        p = jnp.exp(s - m_loc)
        l_loc = jnp.sum(p, axis=1, keepdims=True)
        vw = v_ref[pl.ds(k0, WIN_ROWS * GRID_W), cols]
        o_loc = jnp.dot(p.astype(BF16), vw, preferred_element_type=F32)
        m_all = jnp.maximum(m_loc, m_meta[rows])
        a_loc = jnp.exp(m_loc - m_all)
        a_meta = jnp.exp(m_meta[rows] - m_all)
        o = (a_loc * o_loc + a_meta * o_meta[rows]) / (a_loc * l_loc + a_meta * l_meta[rows])
        o_ref[rows, cols] = o.astype(o_ref.dtype)

    ahead = 2
    pending = [scores(*units[u]) for u in range(ahead)]
    meta = [meta_part(hh) for hh in range(HEADS_PER_STEP)]
    for u, (rr, hh) in enumerate(units):
        if u + ahead < len(units):
            pending.append(scores(*units[u + ahead]))
        finish(rr, hh, pending.pop(0))


def _attn_meta_kernel(q_ref, k_ref, v_ref, mb_ref, att_in_ref, o_ref, *, n_heads, hd_dim):
    del att_in_ref
    scale = hd_dim ** -0.5
    for h in range(n_heads):
        cols = slice(h * hd_dim, (h + 1) * hd_dim)
        s = lax.dot_general(q_ref[:, cols], k_ref[:, cols], (((1,), (1,)), ((), ())),
                            preferred_element_type=F32) * scale + mb_ref[h]
        p = jnp.exp(s - jnp.max(s, axis=1, keepdims=True))
        denom = jnp.sum(p, axis=1, keepdims=True)
        o = jnp.dot(p.astype(BF16), v_ref[:, cols], preferred_element_type=F32)
        o_ref[:, cols] = (o / denom).astype(o_ref.dtype)


def _attention_bias(rel_bias):
    c = jnp.arange(GRID_W)
    cs = jnp.clip(c - WIN_COLS // 2, 0, GRID_W - WIN_COLS)
    in_win = (c[None, :] >= cs[:, None]) & (c[None, :] < cs[:, None] + WIN_COLS)
    dj = jnp.clip(c[None, :] - c[:, None] + WIN_COLS - 1, 0, 2 * WIN_COLS - 2)
    e = jnp.where(in_win[None, None], rel_bias[:, :, dj].astype(F32), NEG)
    di = jnp.arange(WIN_ROWS)[:, None] + jnp.arange(WIN_ROWS)[None, :]
    bv = e[:, di]
    bv = bv.transpose(1, 0, 3, 2, 4)
    return bv.reshape(WIN_ROWS, rel_bias.shape[0], GRID_W, WIN_ROWS * GRID_W)


def _attention(qkv, rel_bias, meta_bias, n_tok, n_meta, n_heads, hd_dim, out_width):
    t_rows = qkv.shape[0]
    n_rows = n_tok // GRID_W
    width = n_heads * hd_dim
    gw = HEADS_PER_STEP * hd_dim
    n_groups = n_heads // HEADS_PER_STEP
    q_tile = ROWS_PER_STEP * GRID_W
    bias = _attention_bias(rel_bias)
    mb = meta_bias.astype(F32).reshape(n_heads, 1, n_meta)
    meta_blk = n_tok // n_meta

    kern = functools.partial(_attn_kernel, n_rows=n_rows, hd_dim=hd_dim)
    att = pl.pallas_call(
        kern,
        grid=(n_groups, n_rows // ROWS_PER_STEP),
        in_specs=[pl.BlockSpec((q_tile, gw), lambda g, b: (b, g)),
                  pl.BlockSpec((n_tok, gw), lambda g, b: (0, n_groups + g)),
                  pl.BlockSpec((n_tok, gw), lambda g, b: (0, 2 * n_groups + g)),
                  pl.BlockSpec((n_meta, gw), lambda g, b: (meta_blk, n_groups + g)),
                  pl.BlockSpec((n_meta, gw), lambda g, b: (meta_blk, 2 * n_groups + g)),
                  pl.BlockSpec((WIN_ROWS, HEADS_PER_STEP, GRID_W, WIN_ROWS * GRID_W),
                               lambda g, b: (0, g, 0, 0)),
                  pl.BlockSpec((HEADS_PER_STEP, 1, n_meta), lambda g, b: (g, 0, 0))],
        out_specs=pl.BlockSpec((q_tile, gw), lambda g, b: (b, g)),
        out_shape=jax.ShapeDtypeStruct((t_rows, out_width), BF16),
        name="attn",
        compiler_params=_params("parallel", "arbitrary"),
    )(qkv, qkv, qkv, qkv, qkv, bias, mb)

    meta_kern = functools.partial(_attn_meta_kernel, n_heads=n_heads, hd_dim=hd_dim)
    return pl.pallas_call(
        meta_kern,
        grid=(1,),
        in_specs=[pl.BlockSpec((n_meta, width), lambda i: (meta_blk, 0)),
                  pl.BlockSpec((n_meta, width), lambda i: (meta_blk, 1)),
                  pl.BlockSpec((n_meta, width), lambda i: (meta_blk, 2)),
                  pl.BlockSpec((n_heads, 1, n_meta), lambda i: (0, 0, 0)),
                  pl.BlockSpec(memory_space=pl.ANY)],
        out_specs=pl.BlockSpec((n_meta, width), lambda i: (meta_blk, 0)),
        out_shape=jax.ShapeDtypeStruct((t_rows, out_width), BF16),
        input_output_aliases={4: 0},
        name="attn_meta",
        compiler_params=_params("arbitrary"),
    )(qkv, qkv, qkv, mb, att)


def kernel(x, meta_tokens, norm_mix_g, w_in, gate_bias, conv_w, conv_b, rel_bias, meta_bias,
           mlstm_norm_g, w_out, norm_ffn_g, w_gate, w_up, w_down, final_norm_g):
    batch, n_tok, d_model = x.shape
    assert batch == 1, "written for a single sequence"
    n_meta = meta_tokens.shape[0]
    depth = w_in.shape[0]
    t_rows = n_tok + n_meta
    att_heads = rel_bias.shape[1]
    n_gates = gate_bias.shape[1]
    ml_heads = n_gates // 4
    qk_width = conv_w.shape[2] // 2
    v_width = mlstm_norm_g.shape[1]
    att_width = (w_in.shape[2] - 2 * qk_width - 2 * v_width - n_gates) // 3
    hd_dim = att_width // att_heads
    dk = qk_width // ml_heads
    dv = v_width // ml_heads
    assert n_tok % (GRID_W * ROWS_PER_STEP) == 0 and n_meta % 16 == 0 and n_tok % n_meta == 0
    assert att_heads % HEADS_PER_STEP == 0

    tm = _divisor_tile(t_rows, 704, 16)
    tm_big = _divisor_tile(t_rows, 3328, 16)
    tr = _divisor_tile(t_rows, 448, 16)
    conv_tile = _divisor_tile(n_tok, 512, CHUNK)

    seg_widths = (3 * att_width, 2 * qk_width, v_width, v_width)
    seg_dtypes = (BF16, F32, BF16, F32)
    n_seg_cols = sum(seg_widths)
    q_scale = jnp.concatenate([jnp.full((qk_width,), dk ** -0.5, F32), jnp.ones((qk_width,), F32)])

    w_in_b = w_in.astype(BF16)
    w_gates_b = w_in_b[:, :, n_seg_cols:]
    w_out_b = w_out.astype(BF16)
    w_down_b = w_down.astype(BF16)

    h = jnp.concatenate([x[0], meta_tokens.astype(x.dtype)], axis=0)
    for l in range(depth):
        u = _rmsnorm(h, norm_mix_g[l], t_rows, BF16, tr)
        qkv, mqk, mv, mo, g = _in_proj(u, w_in_b, w_gates_b, gate_bias[l], l,
                                       seg_widths, seg_dtypes, tm)

        mix = _attention(qkv, rel_bias[l], meta_bias[l], n_tok, n_meta, att_heads, hd_dim,
                         att_width + v_width)
        qk = _qk_conv(mqk, conv_w[l], conv_b[l], q_scale, n_tok, n_meta, conv_tile)
        hf, hb = _mlstm_scan(qk, mv, g, n_tok, n_meta, ml_heads, dk, dv)
        mix = _mlstm_out(hf, hb, mo, mlstm_norm_g[l], mix, ml_heads, dv, tr)

        h = _matmul_residual(mix, w_out_b, l, h, tm, 1024, "out_proj")
        z = _rmsnorm(h, norm_ffn_g[l], t_rows, BF16, tr)
        act = _swiglu(z, w_gate, w_up, l, tm_big, 256)
        h = _matmul_residual(act, w_down_b, l, h, tm, 256, "down_proj")
    y = _rmsnorm(h, final_norm_g, n_tok, x.dtype, _divisor_tile(n_tok, 512, 16))
    return y[None]
```

```python
import functools

import jax
import jax.numpy as jnp
from jax import lax
from jax.experimental import pallas as pl
from jax.experimental.pallas import tpu as pltpu

F32 = jnp.float32
BF16 = jnp.bfloat16

GRID_W = 64
WIN_ROWS = 8
WIN_COLS = 16
CHUNK = 64
RMS_EPS = 1e-6
LANES = 128
LOG2E = 1.4426950408889634
NEG = -1e30
ROWS_PER_STEP = 16
HEADS_PER_STEP = 2

V7X_VMEM_LIMIT = 56 * 1024 * 1024


def _params(*sem):
    return pltpu.CompilerParams(dimension_semantics=sem, vmem_limit_bytes=V7X_VMEM_LIMIT)


def _divisor_tile(n, target, mult):
    best = None
    for d in range(mult, min(n, target) + 1, mult):
        if n % d == 0:
            best = d
    return best if best is not None else n


def _sigmoid(x):
    return 1.0 / (1.0 + jnp.exp(-x))


def _rmsnorm_kernel(x_ref, g_ref, o_ref):
    x = x_ref[...]
    ms = jnp.mean(x * x, axis=-1, keepdims=True)
    o_ref[...] = (x * lax.rsqrt(ms + RMS_EPS) * g_ref[...]).astype(o_ref.dtype)


def _rmsnorm(x, g, out_rows, out_dtype, tile):
    d = x.shape[1]
    return pl.pallas_call(
        _rmsnorm_kernel,
        grid=(out_rows // tile,),
        in_specs=[pl.BlockSpec((tile, d), lambda i: (i, 0)),
                  pl.BlockSpec((1, d), lambda i: (0, 0))],
        out_specs=pl.BlockSpec((tile, d), lambda i: (i, 0)),
        out_shape=jax.ShapeDtypeStruct((out_rows, d), out_dtype),
        name="rmsnorm",
        compiler_params=_params("parallel"),
    )(x, g.reshape(1, d))


def _in_proj_kernel(u_ref, w_ref, wg_ref, bg_ref, *out_refs, bounds):
    j = pl.program_id(1)
    seg_refs, g_ref = out_refs[:-1], out_refs[-1]
    for s, o_ref in enumerate(seg_refs):
        @pl.when(jnp.logical_and(j >= bounds[s], j < bounds[s + 1]))
        def _(o_ref=o_ref):
            o_ref[...] = jnp.dot(u_ref[...], w_ref[...], preferred_element_type=F32).astype(o_ref.dtype)

    @pl.when(j == 0)
    def _():
        ng = g_ref.shape[1]
        g_ref[...] = jnp.dot(u_ref[...], wg_ref[:, :ng], preferred_element_type=F32) + bg_ref[...]


def _in_proj(u, w, gate_bias, layer, seg_widths, seg_dtypes, tm):
    m, k = u.shape
    ng = gate_bias.shape[0]
    gate_col0 = sum(seg_widths)
    assert gate_col0 % LANES == 0 and ng <= LANES and gate_col0 + ng == w.shape[2]
    tn = next(t for t in (1024, 512, 256, 128) if all(sw % t == 0 for sw in seg_widths))
    bounds = [0]
    for sw in seg_widths:
        bounds.append(bounds[-1] + sw // tn)

    def seg_map(s):
        return lambda i, j: (i, jnp.clip(j - bounds[s], 0, bounds[s + 1] - bounds[s] - 1))

    kern = functools.partial(_in_proj_kernel, bounds=tuple(bounds))
    return pl.pallas_call(
        kern,
        grid=(m // tm, bounds[-1]),
        in_specs=[pl.BlockSpec((tm, k), lambda i, j: (i, 0)),
                  pl.BlockSpec((None, k, tn), lambda i, j: (layer, 0, j)),
                  pl.BlockSpec((None, k, LANES), lambda i, j: (layer, 0, gate_col0 // LANES)),
                  pl.BlockSpec((1, ng), lambda i, j: (0, 0))],
        out_specs=[pl.BlockSpec((tm, tn), seg_map(s)) for s in range(len(seg_widths))]
        + [pl.BlockSpec((tm, ng), lambda i, j: (i, 0))],
        out_shape=[jax.ShapeDtypeStruct((m, sw), dt) for sw, dt in zip(seg_widths, seg_dtypes)]
        + [jax.ShapeDtypeStruct((m, ng), F32)],
        name="in_proj",
        compiler_params=_params("parallel", "arbitrary"),
    )(u, w, w, gate_bias.reshape(1, ng))


def _mm_res_kernel(a_ref, w_ref, r_ref, o_ref):
    o_ref[...] = r_ref[...] + jnp.dot(a_ref[...], w_ref[...], preferred_element_type=F32)


def _matmul_residual(a, w, layer, res, tm, tn, name):
    m, k = a.shape
    n = w.shape[2]
    tn = min(tn, n)
    return pl.pallas_call(
        _mm_res_kernel,
        grid=(m // tm, pl.cdiv(n, tn)),
        in_specs=[pl.BlockSpec((tm, k), lambda i, j: (i, 0)),
                  pl.BlockSpec((None, k, tn), lambda i, j: (layer, 0, j)),
                  pl.BlockSpec((tm, tn), lambda i, j: (i, j))],
        out_specs=pl.BlockSpec((tm, tn), lambda i, j: (i, j)),
        out_shape=jax.ShapeDtypeStruct((m, n), F32),
        name=name,
        compiler_params=_params("parallel", "arbitrary"),
    )(a, w, res)


def _swiglu_kernel(a_ref, wg_ref, wu_ref, o_ref):
    a = a_ref[...]
    g = jnp.dot(a, wg_ref[...].astype(a.dtype), preferred_element_type=F32)
    u = jnp.dot(a, wu_ref[...].astype(a.dtype), preferred_element_type=F32)
    o_ref[...] = (g * _sigmoid(g) * u).astype(o_ref.dtype)


def _swiglu(a, wg, wu, layer, tm, tn):
    m, k = a.shape
    n = wg.shape[2]
    tn = min(tn, n)
    return pl.pallas_call(
        _swiglu_kernel,
        grid=(m // tm, pl.cdiv(n, tn)),
        in_specs=[pl.BlockSpec((tm, k), lambda i, j: (i, 0), pipeline_mode=pl.Buffered(1)),
                  pl.BlockSpec((None, k, tn), lambda i, j: (layer, 0, j)),
                  pl.BlockSpec((None, k, tn), lambda i, j: (layer, 0, j))],
        out_specs=pl.BlockSpec((tm, tn), lambda i, j: (i, j)),
        out_shape=jax.ShapeDtypeStruct((m, n), BF16),
        name="swiglu",
        compiler_params=_params("parallel", "arbitrary"),
    )(a, wg, wu)


def _conv_kernel(x_ref, prev_ref, next_ref, w_ref, b_ref, s_ref, o_ref, ext_ref, *, n_meta, half):
    i = pl.program_id(0)
    n = pl.num_programs(0)
    tile = x_ref.shape[0]
    is_meta = i == n - 1
    row = lax.broadcasted_iota(jnp.int32, x_ref.shape, 0)
    x = jnp.where(jnp.logical_and(is_meta, row >= n_meta), 0.0, x_ref[...])
    ext_ref[0:8, :] = jnp.where(is_meta, 0.0, prev_ref[...])
    ext_ref[8:8 + tile, :] = x
    ext_ref[8 + tile:16 + tile, :] = jnp.where(i == n - 2, 0.0, next_ref[...])

    width = 2 * half + 1

    def conv_rows(start, rows):
        y = b_ref[...] + w_ref[0:1, :] * ext_ref[pl.ds(8 + start - half, rows), :]
        for j in range(1, width):
            y = y + w_ref[j:j + 1, :] * ext_ref[pl.ds(8 + start - half + j, rows), :]
        return y

    def finish(y):
        return (y * _sigmoid(y) * s_ref[...]).astype(o_ref.dtype)

    out = finish(conv_rows(0, tile))
    o_ref[...] = jnp.where(jnp.logical_and(is_meta, row >= n_meta), jnp.zeros_like(out), out)

    @pl.when(is_meta)
    def _():
        base = n_meta - 8
        y = conv_rows(base, 8)
        r8 = lax.broadcasted_iota(jnp.int32, y.shape, 0) + base
        for t in range(half):
            for j in range(half + 1 + t, width):
                src = j - half - 1 - t
                y = y + jnp.where(r8 == n_meta - 1 - t,
                                  w_ref[j:j + 1, :] * next_ref[src:src + 1, :], 0.0)
        o_ref[base:base + 8, :] = finish(y)


def _qk_conv(x, w, b, scale, n_tok, n_meta, tile):
    c = x.shape[1]
    n_grid_tiles = n_tok // tile
    kw = w.shape[0]
    half = kw // 2
    t8 = tile // 8
    last8 = (n_tok + n_meta) // 8 - 1

    def prev_map(i):
        return (jnp.where(i == 0, last8, jnp.minimum(i, n_grid_tiles) * t8 - 1), 0)

    def next_map(i):
        return (jnp.where(i >= n_grid_tiles - 1, 0, (i + 1) * t8), 0)

    kern = functools.partial(_conv_kernel, n_meta=n_meta, half=half)
    return pl.pallas_call(
        kern,
        grid=(n_grid_tiles + 1,),
        in_specs=[pl.BlockSpec((tile, c), lambda i: (i, 0)),
                  pl.BlockSpec((8, c), prev_map),
                  pl.BlockSpec((8, c), next_map),
                  pl.BlockSpec((kw, c), lambda i: (0, 0)),
                  pl.BlockSpec((1, c), lambda i: (0, 0)),
                  pl.BlockSpec((1, c), lambda i: (0, 0))],
        out_specs=pl.BlockSpec((tile, c), lambda i: (i, 0)),
        out_shape=jax.ShapeDtypeStruct((n_tok + CHUNK, c), BF16),
        scratch_shapes=[pltpu.VMEM((tile + 16, c), F32)],
        name="mlstm_qk_conv",
        compiler_params=_params("parallel"),
    )(x, x, x, w, b.reshape(1, c), scale.reshape(1, c))


def _split_cumsum(mask, x, mask_first):
    m = mask.astype(BF16)
    total = None
    rest = x
    for _ in range(3):
        part = rest.astype(BF16)
        rest = rest - part.astype(F32)
        term = (jnp.dot(m, part, preferred_element_type=F32) if mask_first
                else jnp.dot(part, m, preferred_element_type=F32))
        total = term if total is None else total + term
    return total


def _scan_kernel(qkf_ref, qkb_ref, vf_ref, vb_ref, gf_ref, gb_ref,
                 hf_ref, hb_ref, c_ref, n_ref, m_ref, *, n_heads, dk, dv, n_meta):
    j = pl.program_id(0)
    nsteps = pl.num_programs(0)
    L = CHUNK

    @pl.when(j == 0)
    def _():
        c_ref[...] = jnp.zeros_like(c_ref)
        n_ref[...] = jnp.zeros_like(n_ref)
        m_ref[...] = jnp.zeros_like(m_ref)

    t_idx = lax.broadcasted_iota(jnp.int32, (L, L), 0)
    s_idx = lax.broadcasted_iota(jnp.int32, (L, L), 1)
    lower = t_idx >= s_idx
    upper = t_idx <= s_idx
    row_l = lax.broadcasted_iota(jnp.int32, (L, 1), 0)

    gate_col = lax.broadcasted_iota(jnp.int32, (L, 4 * n_heads), 1)
    is_forget = (gate_col // n_heads) % 2 == 1

    dirs = (
        (qkf_ref, vf_ref, gf_ref, hf_ref, lower, lower, upper, j == 0),
        (qkb_ref, vb_ref, gb_ref, hb_ref, upper, upper, lower, j == nsteps - 1),
    )
    chains = []
    for d, (qk_ref, v_ref, g_ref, h_ref, mask, cum_col, cum_row, is_meta) in enumerate(dirs):
        valid = jnp.logical_or(jnp.logical_not(is_meta), row_l < n_meta)
        pre = g_ref[...]
        log_sig = jnp.minimum(pre, 0.0) - jnp.log(1.0 + jnp.exp(-jnp.abs(pre)))
        g = jnp.where(valid, jnp.where(is_forget, log_sig, pre), jnp.where(is_forget, 0.0, NEG))
        gt = g.T
        c0 = 2 * d * n_heads
        ig_cols = g[:, c0:c0 + n_heads]
        lf_cols = g[:, c0 + n_heads:c0 + 2 * n_heads]
        ig_rows = gt[c0:c0 + n_heads, :]
        lf_rows = gt[c0 + n_heads:c0 + 2 * n_heads, :]
        b_cols = _split_cumsum(cum_col, lf_cols, True)
        b_rows = _split_cumsum(cum_row, lf_rows, False)
        b_tots = jnp.sum(lf_rows, axis=1, keepdims=True)

        for hd in range(n_heads):
            ch = d * n_heads + hd
            q = qk_ref[:, hd * dk:(hd + 1) * dk]
            k = qk_ref[:, (n_heads + hd) * dk:(n_heads + hd + 1) * dk]
            v = v_ref[:, hd * dv:(hd + 1) * dv]
            v = jnp.where(valid, v, jnp.zeros_like(v))
            b_col = b_cols[:, hd:hd + 1]
            b_row = b_rows[hd:hd + 1, :]
            ig_col = ig_cols[:, hd:hd + 1]
            ig_row = ig_rows[hd:hd + 1, :]
            b_tot = b_tots[hd:hd + 1, :]
            m = m_ref[ch]

            log_d = jnp.where(mask, b_col - b_row + ig_row, NEG)
            log_inter = b_col + m
            m_t = jnp.maximum(log_inter, jnp.max(log_d, axis=1, keepdims=True))
            log_w_row = b_tot - b_row + ig_row
            log_w_col = b_tot - b_col + ig_col
            m_new = jnp.maximum(b_tot + m, jnp.max(log_w_row, axis=1, keepdims=True))
            chains.append(dict(
                ch=ch, q=q, k=k, v=v, m_t=m_t, m_new=m_new, h_ref=h_ref,
                cols=slice(hd * dv, (hd + 1) * dv),
                d_mat=jnp.exp(log_d - m_t),
                inter=jnp.exp(log_inter - m_t),
                decay=jnp.exp(b_tot + m - m_new),
                w_col=jnp.exp(log_w_col - m_new)))

    for c in chains:
        c["qk"] = lax.dot_general(c["q"], c["k"], (((1,), (1,)), ((), ())),
                                  preferred_element_type=F32)
        c["qc"] = jnp.dot(c["q"], c_ref[c["ch"]].astype(BF16), preferred_element_type=F32)
    for c in chains:
        c["s"] = c["qk"] * c["d_mat"]
        c["wv"] = (c["w_col"] * c["v"].astype(F32)).astype(BF16)
        qn = jnp.sum(c["q"].astype(F32) * n_ref[c["ch"]], axis=1, keepdims=True)
        den = c["inter"] * qn + jnp.sum(c["s"], axis=1, keepdims=True)
        c["scale"] = 1.0 / jnp.maximum(jnp.abs(den), jnp.exp(-c["m_t"]))
    for c in chains:
        ch, decay = c["ch"], c["decay"]
        sv = jnp.dot(c["s"].astype(BF16), c["v"], preferred_element_type=F32)
        c["h_ref"][:, c["cols"]] = (c["inter"] * c["qc"] + sv) * c["scale"]
        d_c = lax.dot_general(c["k"], c["wv"], (((0,), (0,)), ((), ())),
                              preferred_element_type=F32)
        c_ref[ch] = decay * c_ref[ch] + d_c
        n_ref[ch] = decay * n_ref[ch] + jnp.sum(c["w_col"] * c["k"].astype(F32), axis=0,
                                                keepdims=True)
        m_ref[ch] = c["m_new"]


def _mlstm_scan(qk, v, g, n_tok, n_meta, n_heads, dk, dv):
    t_rows = v.shape[0]
    n_chunks = n_tok // CHUNK + 1
    ng = g.shape[1]

    def fwd(j):
        return (j + n_chunks - 1) % n_chunks

    def bwd(j):
        return (2 * (n_chunks - 1) - j) % n_chunks

    kern = functools.partial(_scan_kernel, n_heads=n_heads, dk=dk, dv=dv, n_meta=n_meta)
    return pl.pallas_call(
        kern,
        grid=(n_chunks,),
        in_specs=[pl.BlockSpec((CHUNK, qk.shape[1]), lambda j: (fwd(j), 0)),
                  pl.BlockSpec((CHUNK, qk.shape[1]), lambda j: (bwd(j), 0)),
                  pl.BlockSpec((CHUNK, v.shape[1]), lambda j: (fwd(j), 0)),
                  pl.BlockSpec((CHUNK, v.shape[1]), lambda j: (bwd(j), 0)),
                  pl.BlockSpec((CHUNK, ng), lambda j: (fwd(j), 0)),
                  pl.BlockSpec((CHUNK, ng), lambda j: (bwd(j), 0))],
        out_specs=[pl.BlockSpec((CHUNK, v.shape[1]), lambda j: (fwd(j), 0)),
                   pl.BlockSpec((CHUNK, v.shape[1]), lambda j: (bwd(j), 0))],
        out_shape=[jax.ShapeDtypeStruct((t_rows, v.shape[1]), F32),
                   jax.ShapeDtypeStruct((t_rows, v.shape[1]), F32)],
        scratch_shapes=[pltpu.VMEM((2 * n_heads, dk, dv), F32),
                        pltpu.VMEM((2 * n_heads, 1, dk), F32),
                        pltpu.VMEM((2 * n_heads, 1, 1), F32)],
        name="mlstm_scan",
        compiler_params=_params("arbitrary"),
    )(qk, qk, v, v, g, g)


def _mlstm_out_kernel(hf_ref, hb_ref, o_ref, g_ref, mix_in_ref, out_ref, *, n_heads, dv):
    del mix_in_ref
    for hd in range(n_heads):
        sl = slice(hd * dv, (hd + 1) * dv)
        h = hf_ref[:, sl] + hb_ref[:, sl]
        ms = jnp.mean(h * h, axis=-1, keepdims=True)
        h = h * lax.rsqrt(ms + RMS_EPS)
        out_ref[:, sl] = (h * g_ref[:, sl] * _sigmoid(o_ref[:, sl])).astype(out_ref.dtype)


def _mlstm_out(hf, hb, o_pre, norm_g, mix, n_heads, dv, tile):
    t_rows, width = hf.shape
    col_blk = (mix.shape[1] - width) // width
    assert col_blk * width == mix.shape[1] - width
    kern = functools.partial(_mlstm_out_kernel, n_heads=n_heads, dv=dv)
    spec = pl.BlockSpec((tile, width), lambda i: (i, 0))
    return pl.pallas_call(
        kern,
        grid=(t_rows // tile,),
        in_specs=[spec, spec, spec, pl.BlockSpec((1, width), lambda i: (0, 0)),
                  pl.BlockSpec(memory_space=pl.ANY)],
        out_specs=pl.BlockSpec((tile, width), lambda i: (i, col_blk)),
        out_shape=jax.ShapeDtypeStruct(mix.shape, mix.dtype),
        input_output_aliases={4: 0},
        name="mlstm_out",
        compiler_params=_params("parallel"),
    )(hf, hb, o_pre, norm_g.reshape(1, width), mix)


def _attn_kernel(q_ref, k_ref, v_ref, km_ref, vm_ref, bias_ref, mb_ref, o_ref, *, n_rows, hd_dim):
    blk = pl.program_id(1)
    scale = hd_dim ** -0.5 * LOG2E
    nt = (((1,), (1,)), ((), ()))

    def meta_part(hh):
        cols = slice(hh * hd_dim, (hh + 1) * hd_dim)
        sm = lax.dot_general(q_ref[:, cols], km_ref[:, cols], nt,
                             preferred_element_type=F32) * scale + mb_ref[hh]
        m_meta = jnp.max(sm, axis=1, keepdims=True)
        pm = jnp.exp2(sm - m_meta)
        l_meta = jnp.sum(pm, axis=1, keepdims=True)
        o_meta = jnp.dot(pm.astype(BF16), vm_ref[:, cols], preferred_element_type=F32)
        return m_meta, l_meta, o_meta

    units = [(rr, hh) for rr in range(ROWS_PER_STEP) for hh in range(HEADS_PER_STEP)]

    def window_start(rr):
        r = blk * ROWS_PER_STEP + rr
        rs = jnp.clip(r - WIN_ROWS // 2, 0, n_rows - WIN_ROWS)
        return pl.multiple_of(rs * GRID_W, GRID_W), rs - r + WIN_ROWS - 1

    def scores(rr, hh):
        cols = slice(hh * hd_dim, (hh + 1) * hd_dim)
        k0, variant = window_start(rr)
        q = q_ref[rr * GRID_W:(rr + 1) * GRID_W, cols]
        kw = k_ref[pl.ds(k0, WIN_ROWS * GRID_W), cols]
        return lax.dot_general(q, kw, nt, preferred_element_type=F32) * scale + bias_ref[variant, hh]

    def finish(rr, hh, s):
        cols = slice(hh * hd_dim, (hh + 1) * hd_dim)
        rows = slice(rr * GRID_W, (rr + 1) * GRID_W)
        k0, _ = window_start(rr)
        m_meta, l_meta, o_meta = meta[hh]
        m_loc = jnp.max(s, axis=1, keepdims=True)
        p = jnp.exp2(s - m_loc)
        l_loc = jnp.sum(p, axis=1, keepdims=True)
        vw = v_ref[pl.ds(k0, WIN_ROWS * GRID_W), cols]
        o_loc = jnp.dot(p.astype(BF16), vw, preferred_element_type=F32)
        m_all = jnp.maximum(m_loc, m_meta[rows])
        a_loc = jnp.exp2(m_loc - m_all)
        a_meta = jnp.exp2(m_meta[rows] - m_all)
        o = (a_loc * o_loc + a_meta * o_meta[rows]) / (a_loc * l_loc + a_meta * l_meta[rows])
        o_ref[rows, cols] = o.astype(o_ref.dtype)

    ahead = 2
    pending = [scores(*units[u]) for u in range(ahead)]
    meta = [meta_part(hh) for hh in range(HEADS_PER_STEP)]
    for u, (rr, hh) in enumerate(units):
        if u + ahead < len(units):
            pending.append(scores(*units[u + ahead]))
        finish(rr, hh, pending.pop(0))


def _attn_meta_kernel(q_ref, k_ref, v_ref, mb_ref, att_in_ref, o_ref, *, n_heads, hd_dim):
    del att_in_ref
    scale = hd_dim ** -0.5
    for h in range(n_heads):
        cols = slice(h * hd_dim, (h + 1) * hd_dim)
        s = lax.dot_general(q_ref[:, cols], k_ref[:, cols], (((1,), (1,)), ((), ())),
                            preferred_element_type=F32) * scale + mb_ref[h]
        p = jnp.exp(s - jnp.max(s, axis=1, keepdims=True))
        denom = jnp.sum(p, axis=1, keepdims=True)
        o = jnp.dot(p.astype(BF16), v_ref[:, cols], preferred_element_type=F32)
        o_ref[:, cols] = (o / denom).astype(o_ref.dtype)


def _attention_bias(rel_bias):
    depth, n_heads = rel_bias.shape[:2]
    c = jnp.arange(GRID_W)
    cs = jnp.clip(c - WIN_COLS // 2, 0, GRID_W - WIN_COLS)
    in_win = (c[None, :] >= cs[:, None]) & (c[None, :] < cs[:, None] + WIN_COLS)
    dj = jnp.clip(c[None, :] - c[:, None] + WIN_COLS - 1, 0, 2 * WIN_COLS - 2)
    e = jnp.where(in_win, rel_bias[..., dj].astype(F32) * LOG2E, NEG)
    di = jnp.arange(WIN_ROWS)[:, None] + jnp.arange(WIN_ROWS)[None, :]
    bv = e[:, :, di]
    bv = bv.transpose(0, 2, 1, 4, 3, 5)
    return bv.reshape(depth, WIN_ROWS, n_heads, GRID_W, WIN_ROWS * GRID_W)


def _attention(qkv, bias, layer, meta_bias, n_tok, n_meta, n_heads, hd_dim, out_width):
    t_rows = qkv.shape[0]
    n_rows = n_tok // GRID_W
    width = n_heads * hd_dim
    gw = HEADS_PER_STEP * hd_dim
    n_groups = n_heads // HEADS_PER_STEP
    q_tile = ROWS_PER_STEP * GRID_W
    mb = meta_bias.astype(F32).reshape(n_heads, 1, n_meta)
    meta_blk = n_tok // n_meta

    kern = functools.partial(_attn_kernel, n_rows=n_rows, hd_dim=hd_dim)
    att = pl.pallas_call(
        kern,
        grid=(n_groups, n_rows // ROWS_PER_STEP),
        in_specs=[pl.BlockSpec((q_tile, gw), lambda g, b: (b, g)),
                  pl.BlockSpec((n_tok, gw), lambda g, b: (0, n_groups + g)),
                  pl.BlockSpec((n_tok, gw), lambda g, b: (0, 2 * n_groups + g)),
                  pl.BlockSpec((n_meta, gw), lambda g, b: (meta_blk, n_groups + g)),
                  pl.BlockSpec((n_meta, gw), lambda g, b: (meta_blk, 2 * n_groups + g)),
                  pl.BlockSpec((None, WIN_ROWS, HEADS_PER_STEP, GRID_W, WIN_ROWS * GRID_W),
                               lambda g, b: (layer, 0, g, 0, 0)),
                  pl.BlockSpec((HEADS_PER_STEP, 1, n_meta), lambda g, b: (g, 0, 0))],
        out_specs=pl.BlockSpec((q_tile, gw), lambda g, b: (b, g)),
        out_shape=jax.ShapeDtypeStruct((t_rows, out_width), BF16),
        name="attn",
        compiler_params=_params("parallel", "arbitrary"),
    )(qkv, qkv, qkv, qkv, qkv, bias, mb * LOG2E)

    meta_kern = functools.partial(_attn_meta_kernel, n_heads=n_heads, hd_dim=hd_dim)
    return pl.pallas_call(
        meta_kern,
        grid=(1,),
        in_specs=[pl.BlockSpec((n_meta, width), lambda i: (meta_blk, 0)),
                  pl.BlockSpec((n_meta, width), lambda i: (meta_blk, 1)),
                  pl.BlockSpec((n_meta, width), lambda i: (meta_blk, 2)),
                  pl.BlockSpec((n_heads, 1, n_meta), lambda i: (0, 0, 0)),
                  pl.BlockSpec(memory_space=pl.ANY)],
        out_specs=pl.BlockSpec((n_meta, width), lambda i: (meta_blk, 0)),
        out_shape=jax.ShapeDtypeStruct((t_rows, out_width), BF16),
        input_output_aliases={4: 0},
        name="attn_meta",
        compiler_params=_params("arbitrary"),
    )(qkv, qkv, qkv, mb, att)


def kernel(x, meta_tokens, norm_mix_g, w_in, gate_bias, conv_w, conv_b, rel_bias, meta_bias,
           mlstm_norm_g, w_out, norm_ffn_g, w_gate, w_up, w_down, final_norm_g):
    batch, n_tok, d_model = x.shape
    assert batch == 1, "written for a single sequence"
    n_meta = meta_tokens.shape[0]
    depth = w_in.shape[0]
    t_rows = n_tok + n_meta
    att_heads = rel_bias.shape[1]
    n_gates = gate_bias.shape[1]
    ml_heads = n_gates // 4
    qk_width = conv_w.shape[2] // 2
    v_width = mlstm_norm_g.shape[1]
    att_width = (w_in.shape[2] - 2 * qk_width - 2 * v_width - n_gates) // 3
    hd_dim = att_width // att_heads
    dk = qk_width // ml_heads
    dv = v_width // ml_heads
    assert n_tok % (GRID_W * ROWS_PER_STEP) == 0 and n_meta % 16 == 0 and n_tok % n_meta == 0
    assert att_heads % HEADS_PER_STEP == 0

    tm = _divisor_tile(t_rows, 704, 16)
    tm_big = _divisor_tile(t_rows, 3328, 16)
    tr = _divisor_tile(t_rows, 448, 16)
    conv_tile = _divisor_tile(n_tok, 512, CHUNK)

    seg_widths = (3 * att_width, 2 * qk_width, v_width, v_width)
    seg_dtypes = (BF16, F32, BF16, F32)
    q_scale = jnp.concatenate([jnp.full((qk_width,), dk ** -0.5, F32), jnp.ones((qk_width,), F32)])

    w_in_b = w_in.astype(BF16)
    w_out_b = w_out.astype(BF16)
    w_down_b = w_down.astype(BF16)
    attn_bias = _attention_bias(rel_bias)

    h = jnp.concatenate([x[0], meta_tokens.astype(x.dtype)], axis=0)
    for l in range(depth):
        u = _rmsnorm(h, norm_mix_g[l], t_rows, BF16, tr)
        qkv, mqk, mv, mo, g = _in_proj(u, w_in_b, gate_bias[l], l, seg_widths, seg_dtypes, tm)

        mix = _attention(qkv, attn_bias, l, meta_bias[l], n_tok, n_meta, att_heads, hd_dim,
                         att_width + v_width)
        qk = _qk_conv(mqk, conv_w[l], conv_b[l], q_scale, n_tok, n_meta, conv_tile)
        hf, hb = _mlstm_scan(qk, mv, g, n_tok, n_meta, ml_heads, dk, dv)
        mix = _mlstm_out(hf, hb, mo, mlstm_norm_g[l], mix, ml_heads, dv, tr)

        h = _matmul_residual(mix, w_out_b, l, h, tm, 1024, "out_proj")
        z = _rmsnorm(h, norm_ffn_g[l], t_rows, BF16, tr)
        act = _swiglu(z, w_gate, w_up, l, tm_big, 256)
        h = _matmul_residual(act, w_down_b, l, h, tm, 256, "down_proj")
    y = _rmsnorm(h, final_norm_g, n_tok, x.dtype, _divisor_tile(n_tok, 512, 16))
    return y[None]
```

```python
import functools

import jax
import jax.numpy as jnp
from jax import lax
from jax.experimental import pallas as pl
from jax.experimental.pallas import tpu as pltpu

F32 = jnp.float32
BF16 = jnp.bfloat16

GRID_W = 64
WIN_ROWS = 8
WIN_COLS = 16
CHUNK = 64
RMS_EPS = 1e-6
LANES = 128
LOG2E = 1.4426950408889634
NEG = -1e30
ROWS_PER_STEP = 16
HEADS_PER_STEP = 2

V7X_VMEM_LIMIT = 56 * 1024 * 1024


def _params(*sem):
    return pltpu.CompilerParams(dimension_semantics=sem, vmem_limit_bytes=V7X_VMEM_LIMIT)


def _divisor_tile(n, target, mult):
    best = None
    for d in range(mult, min(n, target) + 1, mult):
        if n % d == 0:
            best = d
    return best if best is not None else n


def _sigmoid(x):
    return 1.0 / (1.0 + jnp.exp(-x))


def _rmsnorm_kernel(x_ref, g_ref, o_ref):
    x = x_ref[...]
    ms = jnp.mean(x * x, axis=-1, keepdims=True)
    o_ref[...] = (x * lax.rsqrt(ms + RMS_EPS) * g_ref[...]).astype(o_ref.dtype)


def _rmsnorm(x, g, out_rows, out_dtype, tile):
    d = x.shape[1]
    return pl.pallas_call(
        _rmsnorm_kernel,
        grid=(out_rows // tile,),
        in_specs=[pl.BlockSpec((tile, d), lambda i: (i, 0)),
                  pl.BlockSpec((1, d), lambda i: (0, 0))],
        out_specs=pl.BlockSpec((tile, d), lambda i: (i, 0)),
        out_shape=jax.ShapeDtypeStruct((out_rows, d), out_dtype),
        name="rmsnorm",
        compiler_params=_params("parallel"),
    )(x, g.reshape(1, d))


def _in_proj_kernel(u_ref, w_ref, wg_ref, bg_ref, *out_refs, bounds):
    j = pl.program_id(1)
    seg_refs, g_ref = out_refs[:-1], out_refs[-1]
    for s, o_ref in enumerate(seg_refs):
        @pl.when(jnp.logical_and(j >= bounds[s], j < bounds[s + 1]))
        def _(o_ref=o_ref):
            o_ref[...] = jnp.dot(u_ref[...], w_ref[...], preferred_element_type=F32).astype(o_ref.dtype)

    @pl.when(j == 0)
    def _():
        ng = g_ref.shape[1]
        g_ref[...] = jnp.dot(u_ref[...], wg_ref[:, :ng], preferred_element_type=F32) + bg_ref[...]


def _in_proj(u, w, gate_bias, layer, seg_widths, seg_dtypes, tm):
    m, k = u.shape
    ng = gate_bias.shape[0]
    gate_col0 = sum(seg_widths)
    assert gate_col0 % LANES == 0 and ng <= LANES and gate_col0 + ng == w.shape[2]
    tn = next(t for t in (1024, 512, 256, 128) if all(sw % t == 0 for sw in seg_widths))
    bounds = [0]
    for sw in seg_widths:
        bounds.append(bounds[-1] + sw // tn)

    def seg_map(s):
        return lambda i, j: (i, jnp.clip(j - bounds[s], 0, bounds[s + 1] - bounds[s] - 1))

    kern = functools.partial(_in_proj_kernel, bounds=tuple(bounds))
    return pl.pallas_call(
        kern,
        grid=(m // tm, bounds[-1]),
        in_specs=[pl.BlockSpec((tm, k), lambda i, j: (i, 0)),
                  pl.BlockSpec((None, k, tn), lambda i, j: (layer, 0, j)),
                  pl.BlockSpec((None, k, LANES), lambda i, j: (layer, 0, gate_col0 // LANES)),
                  pl.BlockSpec((1, ng), lambda i, j: (0, 0))],
        out_specs=[pl.BlockSpec((tm, tn), seg_map(s)) for s in range(len(seg_widths))]
        + [pl.BlockSpec((tm, ng), lambda i, j: (i, 0))],
        out_shape=[jax.ShapeDtypeStruct((m, sw), dt) for sw, dt in zip(seg_widths, seg_dtypes)]
        + [jax.ShapeDtypeStruct((m, ng), F32)],
        name="in_proj",
        compiler_params=_params("parallel", "arbitrary"),
    )(u, w, w, gate_bias.reshape(1, ng))


def _mm_res_kernel(a_ref, w_ref, r_ref, o_ref):
    o_ref[...] = r_ref[...] + jnp.dot(a_ref[...], w_ref[...], preferred_element_type=F32)


def _matmul_residual(a, w, layer, res, tm, tn, name):
    m, k = a.shape
    n = w.shape[2]
    tn = min(tn, n)
    return pl.pallas_call(
        _mm_res_kernel,
        grid=(m // tm, pl.cdiv(n, tn)),
        in_specs=[pl.BlockSpec((tm, k), lambda i, j: (i, 0)),
                  pl.BlockSpec((None, k, tn), lambda i, j: (layer, 0, j)),
                  pl.BlockSpec((tm, tn), lambda i, j: (i, j))],
        out_specs=pl.BlockSpec((tm, tn), lambda i, j: (i, j)),
        out_shape=jax.ShapeDtypeStruct((m, n), F32),
        name=name,
        compiler_params=_params("parallel", "arbitrary"),
    )(a, w, res)


def _swiglu_kernel(a_ref, wg_ref, wu_ref, o_ref):
    a = a_ref[...]
    g = jnp.dot(a, wg_ref[...].astype(a.dtype), preferred_element_type=F32)
    u = jnp.dot(a, wu_ref[...].astype(a.dtype), preferred_element_type=F32)
    o_ref[...] = (g * _sigmoid(g) * u).astype(o_ref.dtype)


def _swiglu(a, wg, wu, layer, tm, tn):
    m, k = a.shape
    n = wg.shape[2]
    tn = min(tn, n)
    return pl.pallas_call(
        _swiglu_kernel,
        grid=(m // tm, pl.cdiv(n, tn)),
        in_specs=[pl.BlockSpec((tm, k), lambda i, j: (i, 0), pipeline_mode=pl.Buffered(1)),
                  pl.BlockSpec((None, k, tn), lambda i, j: (layer, 0, j)),
                  pl.BlockSpec((None, k, tn), lambda i, j: (layer, 0, j))],
        out_specs=pl.BlockSpec((tm, tn), lambda i, j: (i, j)),
        out_shape=jax.ShapeDtypeStruct((m, n), BF16),
        name="swiglu",
        compiler_params=_params("parallel", "arbitrary"),
    )(a, wg, wu)


def _conv_kernel(x_ref, prev_ref, next_ref, w_ref, b_ref, s_ref, o_ref, ext_ref, *, n_meta, half):
    i = pl.program_id(0)
    n = pl.num_programs(0)
    tile = x_ref.shape[0]
    is_meta = i == n - 1
    row = lax.broadcasted_iota(jnp.int32, x_ref.shape, 0)
    x = jnp.where(jnp.logical_and(is_meta, row >= n_meta), 0.0, x_ref[...])
    ext_ref[0:8, :] = jnp.where(is_meta, 0.0, prev_ref[...])
    ext_ref[8:8 + tile, :] = x
    ext_ref[8 + tile:16 + tile, :] = jnp.where(i == n - 2, 0.0, next_ref[...])

    width = 2 * half + 1

    def conv_rows(start, rows):
        y = b_ref[...] + w_ref[0:1, :] * ext_ref[pl.ds(8 + start - half, rows), :]
        for j in range(1, width):
            y = y + w_ref[j:j + 1, :] * ext_ref[pl.ds(8 + start - half + j, rows), :]
        return y

    def finish(y):
        return (y * _sigmoid(y) * s_ref[...]).astype(o_ref.dtype)

    out = finish(conv_rows(0, tile))
    o_ref[...] = jnp.where(jnp.logical_and(is_meta, row >= n_meta), jnp.zeros_like(out), out)

    @pl.when(is_meta)
    def _():
        base = n_meta - 8
        y = conv_rows(base, 8)
        r8 = lax.broadcasted_iota(jnp.int32, y.shape, 0) + base
        for t in range(half):
            for j in range(half + 1 + t, width):
                src = j - half - 1 - t
                y = y + jnp.where(r8 == n_meta - 1 - t,
                                  w_ref[j:j + 1, :] * next_ref[src:src + 1, :], 0.0)
        o_ref[base:base + 8, :] = finish(y)


def _qk_conv(x, w, b, scale, n_tok, n_meta, tile):
    c = x.shape[1]
    n_grid_tiles = n_tok // tile
    kw = w.shape[0]
    half = kw // 2
    t8 = tile // 8
    last8 = (n_tok + n_meta) // 8 - 1

    def prev_map(i):
        return (jnp.where(i == 0, last8, jnp.minimum(i, n_grid_tiles) * t8 - 1), 0)

    def next_map(i):
        return (jnp.where(i >= n_grid_tiles - 1, 0, (i + 1) * t8), 0)

    kern = functools.partial(_conv_kernel, n_meta=n_meta, half=half)
    return pl.pallas_call(
        kern,
        grid=(n_grid_tiles + 1,),
        in_specs=[pl.BlockSpec((tile, c), lambda i: (i, 0)),
                  pl.BlockSpec((8, c), prev_map),
                  pl.BlockSpec((8, c), next_map),
                  pl.BlockSpec((kw, c), lambda i: (0, 0)),
                  pl.BlockSpec((1, c), lambda i: (0, 0)),
                  pl.BlockSpec((1, c), lambda i: (0, 0))],
        out_specs=pl.BlockSpec((tile, c), lambda i: (i, 0)),
        out_shape=jax.ShapeDtypeStruct((n_tok + CHUNK, c), BF16),
        scratch_shapes=[pltpu.VMEM((tile + 16, c), F32)],
        name="mlstm_qk_conv",
        compiler_params=_params("parallel"),
    )(x, x, x, w, b.reshape(1, c), scale.reshape(1, c))


def _split_cumsum(mask, x, mask_first):
    m = mask.astype(BF16)
    total = None
    rest = x
    for _ in range(3):
        part = rest.astype(BF16)
        rest = rest - part.astype(F32)
        term = (jnp.dot(m, part, preferred_element_type=F32) if mask_first
                else jnp.dot(part, m, preferred_element_type=F32))
        total = term if total is None else total + term
    return total


def _scan_kernel(qkf_ref, qkb_ref, vf_ref, vb_ref, gf_ref, gb_ref,
                 hf_ref, hb_ref, c_ref, n_ref, m_ref, *, n_heads, dk, dv, n_meta):
    j = pl.program_id(0)
    nsteps = pl.num_programs(0)
    L = CHUNK

    @pl.when(j == 0)
    def _():
        c_ref[...] = jnp.zeros_like(c_ref)
        n_ref[...] = jnp.zeros_like(n_ref)
        m_ref[...] = jnp.zeros_like(m_ref)

    t_idx = lax.broadcasted_iota(jnp.int32, (L, L), 0)
    s_idx = lax.broadcasted_iota(jnp.int32, (L, L), 1)
    lower = t_idx >= s_idx
    upper = t_idx <= s_idx
    row_l = lax.broadcasted_iota(jnp.int32, (L, 1), 0)

    gate_col = lax.broadcasted_iota(jnp.int32, (L, 4 * n_heads), 1)
    is_forget = (gate_col // n_heads) % 2 == 1

    dirs = (
        (qkf_ref, vf_ref, gf_ref, hf_ref, lower, lower, upper, j == 0),
        (qkb_ref, vb_ref, gb_ref, hb_ref, upper, upper, lower, j == nsteps - 1),
    )
    chains = []
    for d, (qk_ref, v_ref, g_ref, h_ref, mask, cum_col, cum_row, is_meta) in enumerate(dirs):
        valid = jnp.logical_or(jnp.logical_not(is_meta), row_l < n_meta)
        pre = g_ref[...]
        log_sig = jnp.minimum(pre, 0.0) - jnp.log(1.0 + jnp.exp(-jnp.abs(pre)))
        g = jnp.where(valid, jnp.where(is_forget, log_sig, pre), jnp.where(is_forget, 0.0, NEG))
        gt = g.T
        c0 = 2 * d * n_heads
        ig_cols = g[:, c0:c0 + n_heads]
        lf_cols = g[:, c0 + n_heads:c0 + 2 * n_heads]
        ig_rows = gt[c0:c0 + n_heads, :]
        lf_rows = gt[c0 + n_heads:c0 + 2 * n_heads, :]
        b_cols = _split_cumsum(cum_col, lf_cols, True)
        b_rows = _split_cumsum(cum_row, lf_rows, False)
        b_tots = jnp.sum(lf_rows, axis=1, keepdims=True)

        for hd in range(n_heads):
            ch = d * n_heads + hd
            q = qk_ref[:, hd * dk:(hd + 1) * dk]
            k = qk_ref[:, (n_heads + hd) * dk:(n_heads + hd + 1) * dk]
            v = v_ref[:, hd * dv:(hd + 1) * dv]
            v = jnp.where(valid, v, jnp.zeros_like(v))
            b_col = b_cols[:, hd:hd + 1]
            b_row = b_rows[hd:hd + 1, :]
            ig_col = ig_cols[:, hd:hd + 1]
            ig_row = ig_rows[hd:hd + 1, :]
            b_tot = b_tots[hd:hd + 1, :]
            m = m_ref[ch]

            log_d = jnp.where(mask, b_col - b_row + ig_row, NEG)
            log_inter = b_col + m
            m_t = jnp.maximum(log_inter, jnp.max(log_d, axis=1, keepdims=True))
            log_w_row = b_tot - b_row + ig_row
            log_w_col = b_tot - b_col + ig_col
            m_new = jnp.maximum(b_tot + m, jnp.max(log_w_row, axis=1, keepdims=True))
            chains.append(dict(
                ch=ch, q=q, k=k, v=v, m_t=m_t, m_new=m_new, h_ref=h_ref,
                cols=slice(hd * dv, (hd + 1) * dv),
                d_mat=jnp.exp(log_d - m_t),
                inter=jnp.exp(log_inter - m_t),
                decay=jnp.exp(b_tot + m - m_new),
                w_col=jnp.exp(log_w_col - m_new)))

    for c in chains:
        c["qk"] = lax.dot_general(c["q"], c["k"], (((1,), (1,)), ((), ())),
                                  preferred_element_type=F32)
        c["qc"] = jnp.dot(c["q"], c_ref[c["ch"]].astype(BF16), preferred_element_type=F32)
    for c in chains:
        c["s"] = c["qk"] * c["d_mat"]
        c["wv"] = (c["w_col"] * c["v"].astype(F32)).astype(BF16)
        qn = jnp.sum(c["q"].astype(F32) * n_ref[c["ch"]], axis=1, keepdims=True)
        den = c["inter"] * qn + jnp.sum(c["s"], axis=1, keepdims=True)
        c["scale"] = 1.0 / jnp.maximum(jnp.abs(den), jnp.exp(-c["m_t"]))
    for c in chains:
        ch, decay = c["ch"], c["decay"]
        sv = jnp.dot(c["s"].astype(BF16), c["v"], preferred_element_type=F32)
        c["h_ref"][:, c["cols"]] = (c["inter"] * c["qc"] + sv) * c["scale"]
        d_c = lax.dot_general(c["k"], c["wv"], (((0,), (0,)), ((), ())),
                              preferred_element_type=F32)
        c_ref[ch] = decay * c_ref[ch] + d_c
        n_ref[ch] = decay * n_ref[ch] + jnp.sum(c["w_col"] * c["k"].astype(F32), axis=0,
                                                keepdims=True)
        m_ref[ch] = c["m_new"]


def _mlstm_scan(qk, v, g, n_tok, n_meta, n_heads, dk, dv):
    t_rows = v.shape[0]
    n_chunks = n_tok // CHUNK + 1
    ng = g.shape[1]

    def fwd(j):
        return (j + n_chunks - 1) % n_chunks

    def bwd(j):
        return (2 * (n_chunks - 1) - j) % n_chunks

    kern = functools.partial(_scan_kernel, n_heads=n_heads, dk=dk, dv=dv, n_meta=n_meta)
    return pl.pallas_call(
        kern,
        grid=(n_chunks,),
        in_specs=[pl.BlockSpec((CHUNK, qk.shape[1]), lambda j: (fwd(j), 0)),
                  pl.BlockSpec((CHUNK, qk.shape[1]), lambda j: (bwd(j), 0)),
                  pl.BlockSpec((CHUNK, v.shape[1]), lambda j: (fwd(j), 0)),
                  pl.BlockSpec((CHUNK, v.shape[1]), lambda j: (bwd(j), 0)),
                  pl.BlockSpec((CHUNK, ng), lambda j: (fwd(j), 0)),
                  pl.BlockSpec((CHUNK, ng), lambda j: (bwd(j), 0))],
        out_specs=[pl.BlockSpec((CHUNK, v.shape[1]), lambda j: (fwd(j), 0)),
                   pl.BlockSpec((CHUNK, v.shape[1]), lambda j: (bwd(j), 0))],
        out_shape=[jax.ShapeDtypeStruct((t_rows, v.shape[1]), F32),
                   jax.ShapeDtypeStruct((t_rows, v.shape[1]), F32)],
        scratch_shapes=[pltpu.VMEM((2 * n_heads, dk, dv), F32),
                        pltpu.VMEM((2 * n_heads, 1, dk), F32),
                        pltpu.VMEM((2 * n_heads, 1, 1), F32)],
        name="mlstm_scan",
        compiler_params=_params("arbitrary"),
    )(qk, qk, v, v, g, g)


def _mlstm_out_kernel(hf_ref, hb_ref, o_ref, g_ref, mix_in_ref, out_ref, *, n_heads, dv):
    del mix_in_ref
    for hd in range(n_heads):
        sl = slice(hd * dv, (hd + 1) * dv)
        h = hf_ref[:, sl] + hb_ref[:, sl]
        ms = jnp.mean(h * h, axis=-1, keepdims=True)
        h = h * lax.rsqrt(ms + RMS_EPS)
        out_ref[:, sl] = (h * g_ref[:, sl] * _sigmoid(o_ref[:, sl])).astype(out_ref.dtype)


def _mlstm_out(hf, hb, o_pre, norm_g, mix, n_heads, dv, tile):
    t_rows, width = hf.shape
    col_blk = (mix.shape[1] - width) // width
    assert col_blk * width == mix.shape[1] - width
    kern = functools.partial(_mlstm_out_kernel, n_heads=n_heads, dv=dv)
    spec = pl.BlockSpec((tile, width), lambda i: (i, 0))
    return pl.pallas_call(
        kern,
        grid=(t_rows // tile,),
        in_specs=[spec, spec, spec, pl.BlockSpec((1, width), lambda i: (0, 0)),
                  pl.BlockSpec(memory_space=pl.ANY)],
        out_specs=pl.BlockSpec((tile, width), lambda i: (i, col_blk)),
        out_shape=jax.ShapeDtypeStruct(mix.shape, mix.dtype),
        input_output_aliases={4: 0},
        name="mlstm_out",
        compiler_params=_params("parallel"),
    )(hf, hb, o_pre, norm_g.reshape(1, width), mix)


def _attn_kernel(q_ref, k_ref, v_ref, km_ref, vm_ref, bias_ref, mb_ref, o_ref, *, n_rows, hd_dim):
    blk = pl.program_id(1)
    scale = hd_dim ** -0.5 * LOG2E
    nt = (((1,), (1,)), ((), ()))

    def meta_part(hh):
        cols = slice(hh * hd_dim, (hh + 1) * hd_dim)
        sm = lax.dot_general(q_ref[:, cols], km_ref[:, cols], nt,
                             preferred_element_type=F32) * scale + mb_ref[hh]
        m_meta = jnp.max(sm, axis=1, keepdims=True)
        pm = jnp.exp2(sm - m_meta)
        l_meta = jnp.sum(pm, axis=1, keepdims=True)
        o_meta = jnp.dot(pm.astype(BF16), vm_ref[:, cols], preferred_element_type=F32)
        return m_meta, l_meta, o_meta

    units = [(rr, hh) for rr in range(ROWS_PER_STEP) for hh in range(HEADS_PER_STEP)]

    def window_start(rr):
        r = blk * ROWS_PER_STEP + rr
        rs = jnp.clip(r - WIN_ROWS // 2, 0, n_rows - WIN_ROWS)
        return pl.multiple_of(rs * GRID_W, GRID_W), rs - r + WIN_ROWS - 1

    def scores(rr, hh):
        cols = slice(hh * hd_dim, (hh + 1) * hd_dim)
        k0, variant = window_start(rr)
        q = q_ref[rr * GRID_W:(rr + 1) * GRID_W, cols]
        kw = k_ref[pl.ds(k0, WIN_ROWS * GRID_W), cols]
        return lax.dot_general(q, kw, nt, preferred_element_type=F32) * scale + bias_ref[variant, hh]

    def finish(rr, hh, s):
        cols = slice(hh * hd_dim, (hh + 1) * hd_dim)
        rows = slice(rr * GRID_W, (rr + 1) * GRID_W)
        k0, _ = window_start(rr)
        m_meta, l_meta, o_meta = meta[hh]
        m_loc = jnp.max(s, axis=1, keepdims=True)
        p = jnp.exp2(s - m_loc)
        l_loc = jnp.sum(p, axis=1, keepdims=True)
        vw = v_ref[pl.ds(k0, WIN_ROWS * GRID_W), cols]
        o_loc = jnp.dot(p.astype(BF16), vw, preferred_element_type=F32)
        m_all = jnp.maximum(m_loc, m_meta[rows])
        a_loc = jnp.exp2(m_loc - m_all)
        a_meta = jnp.exp2(m_meta[rows] - m_all)
        o = (a_loc * o_loc + a_meta * o_meta[rows]) / (a_loc * l_loc + a_meta * l_meta[rows])
        o_ref[rows, cols] = o.astype(o_ref.dtype)

    ahead = 4
    pending = [scores(*units[u]) for u in range(ahead)]
    meta = [meta_part(hh) for hh in range(HEADS_PER_STEP)]
    for u, (rr, hh) in enumerate(units):
        if u + ahead < len(units):
            pending.append(scores(*units[u + ahead]))
        finish(rr, hh, pending.pop(0))


def _attn_meta_kernel(q_ref, k_ref, v_ref, mb_ref, att_in_ref, o_ref, *, n_heads, hd_dim):
    del att_in_ref
    scale = hd_dim ** -0.5
    for h in range(n_heads):
        cols = slice(h * hd_dim, (h + 1) * hd_dim)
        s = lax.dot_general(q_ref[:, cols], k_ref[:, cols], (((1,), (1,)), ((), ())),
                            preferred_element_type=F32) * scale + mb_ref[h]
        p = jnp.exp(s - jnp.max(s, axis=1, keepdims=True))
        denom = jnp.sum(p, axis=1, keepdims=True)
        o = jnp.dot(p.astype(BF16), v_ref[:, cols], preferred_element_type=F32)
        o_ref[:, cols] = (o / denom).astype(o_ref.dtype)


def _attention_bias(rel_bias):
    depth, n_heads = rel_bias.shape[:2]
    c = jnp.arange(GRID_W)
    cs = jnp.clip(c - WIN_COLS // 2, 0, GRID_W - WIN_COLS)
    in_win = (c[None, :] >= cs[:, None]) & (c[None, :] < cs[:, None] + WIN_COLS)
    rb = jnp.pad(rel_bias.astype(F32) * LOG2E, ((0, 0),) * 3 + ((GRID_W, GRID_W),))
    e = jnp.stack([rb[..., GRID_W + WIN_COLS - 1 - q:2 * GRID_W + WIN_COLS - 1 - q]
                   for q in range(GRID_W)], axis=3)
    e = jnp.where(in_win, e, NEG)
    bv = jnp.stack([e[:, :, v:v + WIN_ROWS] for v in range(WIN_ROWS)], axis=1)
    bv = bv.transpose(0, 1, 2, 4, 3, 5)
    return bv.reshape(depth, WIN_ROWS, n_heads, GRID_W, WIN_ROWS * GRID_W)


def _attention(qkv, bias, layer, meta_bias, n_tok, n_meta, n_heads, hd_dim, out_width):
    t_rows = qkv.shape[0]
    n_rows = n_tok // GRID_W
    width = n_heads * hd_dim
    gw = HEADS_PER_STEP * hd_dim
    n_groups = n_heads // HEADS_PER_STEP
    q_tile = ROWS_PER_STEP * GRID_W
    mb = meta_bias.astype(F32).reshape(n_heads, 1, n_meta)
    meta_blk = n_tok // n_meta

    kern = functools.partial(_attn_kernel, n_rows=n_rows, hd_dim=hd_dim)
    att = pl.pallas_call(
        kern,
        grid=(n_groups, n_rows // ROWS_PER_STEP),
        in_specs=[pl.BlockSpec((q_tile, gw), lambda g, b: (b, g)),
                  pl.BlockSpec((n_tok, gw), lambda g, b: (0, n_groups + g)),
                  pl.BlockSpec((n_tok, gw), lambda g, b: (0, 2 * n_groups + g)),
                  pl.BlockSpec((n_meta, gw), lambda g, b: (meta_blk, n_groups + g)),
                  pl.BlockSpec((n_meta, gw), lambda g, b: (meta_blk, 2 * n_groups + g)),
                  pl.BlockSpec((None, WIN_ROWS, HEADS_PER_STEP, GRID_W, WIN_ROWS * GRID_W),
                               lambda g, b: (layer, 0, g, 0, 0)),
                  pl.BlockSpec((HEADS_PER_STEP, 1, n_meta), lambda g, b: (g, 0, 0))],
        out_specs=pl.BlockSpec((q_tile, gw), lambda g, b: (b, g)),
        out_shape=jax.ShapeDtypeStruct((t_rows, out_width), BF16),
        name="attn",
        compiler_params=_params("parallel", "arbitrary"),
    )(qkv, qkv, qkv, qkv, qkv, bias, mb * LOG2E)

    meta_kern = functools.partial(_attn_meta_kernel, n_heads=n_heads, hd_dim=hd_dim)
    return pl.pallas_call(
        meta_kern,
        grid=(1,),
        in_specs=[pl.BlockSpec((n_meta, width), lambda i: (meta_blk, 0)),
                  pl.BlockSpec((n_meta, width), lambda i: (meta_blk, 1)),
                  pl.BlockSpec((n_meta, width), lambda i: (meta_blk, 2)),
                  pl.BlockSpec((n_heads, 1, n_meta), lambda i: (0, 0, 0)),
                  pl.BlockSpec(memory_space=pl.ANY)],
        out_specs=pl.BlockSpec((n_meta, width), lambda i: (meta_blk, 0)),
        out_shape=jax.ShapeDtypeStruct((t_rows, out_width), BF16),
        input_output_aliases={4: 0},
        name="attn_meta",
        compiler_params=_params("arbitrary"),
    )(qkv, qkv, qkv, mb, att)


def kernel(x, meta_tokens, norm_mix_g, w_in, gate_bias, conv_w, conv_b, rel_bias, meta_bias,
           mlstm_norm_g, w_out, norm_ffn_g, w_gate, w_up, w_down, final_norm_g):
    batch, n_tok, d_model = x.shape
    assert batch == 1, "written for a single sequence"
    n_meta = meta_tokens.shape[0]
    depth = w_in.shape[0]
    t_rows = n_tok + n_meta
    att_heads = rel_bias.shape[1]
    n_gates = gate_bias.shape[1]
    ml_heads = n_gates // 4
    qk_width = conv_w.shape[2] // 2
    v_width = mlstm_norm_g.shape[1]
    att_width = (w_in.shape[2] - 2 * qk_width - 2 * v_width - n_gates) // 3
    hd_dim = att_width // att_heads
    dk = qk_width // ml_heads
    dv = v_width // ml_heads
    assert n_tok % (GRID_W * ROWS_PER_STEP) == 0 and n_meta % 16 == 0 and n_tok % n_meta == 0
    assert att_heads % HEADS_PER_STEP == 0

    tm = _divisor_tile(t_rows, 704, 16)
    tm_big = _divisor_tile(t_rows, 3328, 16)
    tr = tm
    conv_tile = _divisor_tile(n_tok, 512, CHUNK)

    seg_widths = (3 * att_width, 2 * qk_width, v_width, v_width)
    seg_dtypes = (BF16, F32, BF16, F32)
    q_scale = jnp.concatenate([jnp.full((qk_width,), dk ** -0.5, F32), jnp.ones((qk_width,), F32)])

    w_in_b = w_in.astype(BF16)
    w_out_b = w_out.astype(BF16)
    w_down_b = w_down.astype(BF16)
    attn_bias = _attention_bias(rel_bias)

    h = jnp.concatenate([x[0], meta_tokens.astype(x.dtype)], axis=0)
    for l in range(depth):
        u = _rmsnorm(h, norm_mix_g[l], t_rows, BF16, tr)
        qkv, mqk, mv, mo, g = _in_proj(u, w_in_b, gate_bias[l], l, seg_widths, seg_dtypes, tm)

        mix = _attention(qkv, attn_bias, l, meta_bias[l], n_tok, n_meta, att_heads, hd_dim,
                         att_width + v_width)
        qk = _qk_conv(mqk, conv_w[l], conv_b[l], q_scale, n_tok, n_meta, conv_tile)
        hf, hb = _mlstm_scan(qk, mv, g, n_tok, n_meta, ml_heads, dk, dv)
        mix = _mlstm_out(hf, hb, mo, mlstm_norm_g[l], mix, ml_heads, dv, tr)

        h = _matmul_residual(mix, w_out_b, l, h, tm, 1024, "out_proj")
        z = _rmsnorm(h, norm_ffn_g[l], t_rows, BF16, tr)
        act = _swiglu(z, w_gate, w_up, l, tm_big, 256)
        h = _matmul_residual(act, w_down_b, l, h, tm, 256, "down_proj")
    y = _rmsnorm(h, final_norm_g, n_tok, x.dtype, _divisor_tile(n_tok, 512, 16))
    return y[None]
```

```python
import functools

import jax
import jax.numpy as jnp
from jax import lax
from jax.experimental import pallas as pl
from jax.experimental.pallas import tpu as pltpu

F32 = jnp.float32
BF16 = jnp.bfloat16

GRID_W = 64
WIN_ROWS = 8
WIN_COLS = 16
CHUNK = 64
RMS_EPS = 1e-6
LANES = 128
LOG2E = 1.4426950408889634
NEG = -1e30
ROWS_PER_STEP = 32
HEADS_PER_STEP = 2

V7X_VMEM_LIMIT = 56 * 1024 * 1024


def _params(*sem):
    return pltpu.CompilerParams(dimension_semantics=sem, vmem_limit_bytes=V7X_VMEM_LIMIT)


def _divisor_tile(n, target, mult):
    best = None
    for d in range(mult, min(n, target) + 1, mult):
        if n % d == 0:
            best = d
    return best if best is not None else n


def _sigmoid(x):
    return 1.0 / (1.0 + jnp.exp(-x))


def _rmsnorm_kernel(x_ref, g_ref, o_ref):
    x = x_ref[...]
    ms = jnp.mean(x * x, axis=-1, keepdims=True)
    o_ref[...] = (x * lax.rsqrt(ms + RMS_EPS) * g_ref[...]).astype(o_ref.dtype)


def _rmsnorm(x, g, out_rows, out_dtype, tile):
    d = x.shape[1]
    return pl.pallas_call(
        _rmsnorm_kernel,
        grid=(out_rows // tile,),
        in_specs=[pl.BlockSpec((tile, d), lambda i: (i, 0)),
                  pl.BlockSpec((1, d), lambda i: (0, 0))],
        out_specs=pl.BlockSpec((tile, d), lambda i: (i, 0)),
        out_shape=jax.ShapeDtypeStruct((out_rows, d), out_dtype),
        name="rmsnorm",
        compiler_params=_params("parallel"),
    )(x, g.reshape(1, d))


def _in_proj_kernel(u_ref, w_ref, wg_ref, bg_ref, *out_refs, bounds):
    j = pl.program_id(1)
    seg_refs, g_ref = out_refs[:-1], out_refs[-1]
    for s, o_ref in enumerate(seg_refs):
        @pl.when(jnp.logical_and(j >= bounds[s], j < bounds[s + 1]))
        def _(o_ref=o_ref):
            o_ref[...] = jnp.dot(u_ref[...], w_ref[...], preferred_element_type=F32).astype(o_ref.dtype)

    @pl.when(j == 0)
    def _():
        ng = g_ref.shape[1]
        g_ref[...] = jnp.dot(u_ref[...], wg_ref[:, :ng], preferred_element_type=F32) + bg_ref[...]


def _in_proj(u, w, gate_bias, layer, seg_widths, seg_dtypes, tm):
    m, k = u.shape
    ng = gate_bias.shape[0]
    gate_col0 = sum(seg_widths)
    assert gate_col0 % LANES == 0 and ng <= LANES and gate_col0 + ng == w.shape[2]
    tn = next(t for t in (1024, 512, 256, 128) if all(sw % t == 0 for sw in seg_widths))
    bounds = [0]
    for sw in seg_widths:
        bounds.append(bounds[-1] + sw // tn)

    def seg_map(s):
        return lambda i, j: (i, jnp.clip(j - bounds[s], 0, bounds[s + 1] - bounds[s] - 1))

    kern = functools.partial(_in_proj_kernel, bounds=tuple(bounds))
    return pl.pallas_call(
        kern,
        grid=(m // tm, bounds[-1]),
        in_specs=[pl.BlockSpec((tm, k), lambda i, j: (i, 0)),
                  pl.BlockSpec((None, k, tn), lambda i, j: (layer, 0, j)),
                  pl.BlockSpec((None, k, LANES), lambda i, j: (layer, 0, gate_col0 // LANES)),
                  pl.BlockSpec((1, ng), lambda i, j: (0, 0))],
        out_specs=[pl.BlockSpec((tm, tn), seg_map(s)) for s in range(len(seg_widths))]
        + [pl.BlockSpec((tm, ng), lambda i, j: (i, 0))],
        out_shape=[jax.ShapeDtypeStruct((m, sw), dt) for sw, dt in zip(seg_widths, seg_dtypes)]
        + [jax.ShapeDtypeStruct((m, ng), F32)],
        name="in_proj",
        compiler_params=_params("parallel", "arbitrary"),
    )(u, w, w, gate_bias.reshape(1, ng))


def _mm_res_kernel(a_ref, w_ref, r_ref, o_ref):
    o_ref[...] = r_ref[...] + jnp.dot(a_ref[...], w_ref[...], preferred_element_type=F32)


def _matmul_residual(a, w, layer, res, tm, tn, name):
    m, k = a.shape
    n = w.shape[2]
    tn = min(tn, n)
    return pl.pallas_call(
        _mm_res_kernel,
        grid=(m // tm, pl.cdiv(n, tn)),
        in_specs=[pl.BlockSpec((tm, k), lambda i, j: (i, 0)),
                  pl.BlockSpec((None, k, tn), lambda i, j: (layer, 0, j)),
                  pl.BlockSpec((tm, tn), lambda i, j: (i, j))],
        out_specs=pl.BlockSpec((tm, tn), lambda i, j: (i, j)),
        out_shape=jax.ShapeDtypeStruct((m, n), F32),
        name=name,
        compiler_params=_params("parallel", "arbitrary"),
    )(a, w, res)


def _swiglu_kernel(a_ref, wg_ref, wu_ref, o_ref):
    a = a_ref[...]
    g = jnp.dot(a, wg_ref[...].astype(a.dtype), preferred_element_type=F32)
    u = jnp.dot(a, wu_ref[...].astype(a.dtype), preferred_element_type=F32)
    o_ref[...] = (g * _sigmoid(g) * u).astype(o_ref.dtype)


def _swiglu(a, wg, wu, layer, tm, tn):
    m, k = a.shape
    n = wg.shape[2]
    tn = min(tn, n)
    return pl.pallas_call(
        _swiglu_kernel,
        grid=(m // tm, pl.cdiv(n, tn)),
        in_specs=[pl.BlockSpec((tm, k), lambda i, j: (i, 0), pipeline_mode=pl.Buffered(1)),
                  pl.BlockSpec((None, k, tn), lambda i, j: (layer, 0, j)),
                  pl.BlockSpec((None, k, tn), lambda i, j: (layer, 0, j))],
        out_specs=pl.BlockSpec((tm, tn), lambda i, j: (i, j)),
        out_shape=jax.ShapeDtypeStruct((m, n), BF16),
        name="swiglu",
        compiler_params=_params("parallel", "arbitrary"),
    )(a, wg, wu)


def _conv_kernel(x_ref, prev_ref, next_ref, w_ref, b_ref, s_ref, o_ref, ext_ref, *, n_meta, half):
    i = pl.program_id(0)
    n = pl.num_programs(0)
    tile = x_ref.shape[0]
    is_meta = i == n - 1
    row = lax.broadcasted_iota(jnp.int32, x_ref.shape, 0)
    x = jnp.where(jnp.logical_and(is_meta, row >= n_meta), 0.0, x_ref[...])
    ext_ref[0:8, :] = jnp.where(is_meta, 0.0, prev_ref[...])
    ext_ref[8:8 + tile, :] = x
    ext_ref[8 + tile:16 + tile, :] = jnp.where(i == n - 2, 0.0, next_ref[...])

    width = 2 * half + 1

    def conv_rows(start, rows):
        y = b_ref[...] + w_ref[0:1, :] * ext_ref[pl.ds(8 + start - half, rows), :]
        for j in range(1, width):
            y = y + w_ref[j:j + 1, :] * ext_ref[pl.ds(8 + start - half + j, rows), :]
        return y

    def finish(y):
        return (y * _sigmoid(y) * s_ref[...]).astype(o_ref.dtype)

    out = finish(conv_rows(0, tile))
    o_ref[...] = jnp.where(jnp.logical_and(is_meta, row >= n_meta), jnp.zeros_like(out), out)

    @pl.when(is_meta)
    def _():
        base = n_meta - 8
        y = conv_rows(base, 8)
        r8 = lax.broadcasted_iota(jnp.int32, y.shape, 0) + base
        for t in range(half):
            for j in range(half + 1 + t, width):
                src = j - half - 1 - t
                y = y + jnp.where(r8 == n_meta - 1 - t,
                                  w_ref[j:j + 1, :] * next_ref[src:src + 1, :], 0.0)
        o_ref[base:base + 8, :] = finish(y)


def _qk_conv(x, w, b, scale, n_tok, n_meta, tile):
    c = x.shape[1]
    n_grid_tiles = n_tok // tile
    kw = w.shape[0]
    half = kw // 2
    t8 = tile // 8
    last8 = (n_tok + n_meta) // 8 - 1

    def prev_map(i):
        return (jnp.where(i == 0, last8, jnp.minimum(i, n_grid_tiles) * t8 - 1), 0)

    def next_map(i):
        return (jnp.where(i >= n_grid_tiles - 1, 0, (i + 1) * t8), 0)

    kern = functools.partial(_conv_kernel, n_meta=n_meta, half=half)
    return pl.pallas_call(
        kern,
        grid=(n_grid_tiles + 1,),
        in_specs=[pl.BlockSpec((tile, c), lambda i: (i, 0)),
                  pl.BlockSpec((8, c), prev_map),
                  pl.BlockSpec((8, c), next_map),
                  pl.BlockSpec((kw, c), lambda i: (0, 0)),
                  pl.BlockSpec((1, c), lambda i: (0, 0)),
                  pl.BlockSpec((1, c), lambda i: (0, 0))],
        out_specs=pl.BlockSpec((tile, c), lambda i: (i, 0)),
        out_shape=jax.ShapeDtypeStruct((n_tok + CHUNK, c), BF16),
        scratch_shapes=[pltpu.VMEM((tile + 16, c), F32)],
        name="mlstm_qk_conv",
        compiler_params=_params("parallel"),
    )(x, x, x, w, b.reshape(1, c), scale.reshape(1, c))


def _split_cumsum(mask, x, mask_first):
    m = mask.astype(BF16)
    total = None
    rest = x
    for _ in range(3):
        part = rest.astype(BF16)
        rest = rest - part.astype(F32)
        term = (jnp.dot(m, part, preferred_element_type=F32) if mask_first
                else jnp.dot(part, m, preferred_element_type=F32))
        total = term if total is None else total + term
    return total


def _scan_kernel(qkf_ref, qkb_ref, vf_ref, vb_ref, gf_ref, gb_ref,
                 hf_ref, hb_ref, c_ref, n_ref, m_ref, *, n_heads, dk, dv, n_meta):
    j = pl.program_id(0)
    nsteps = pl.num_programs(0)
    L = CHUNK

    @pl.when(j == 0)
    def _():
        c_ref[...] = jnp.zeros_like(c_ref)
        n_ref[...] = jnp.zeros_like(n_ref)
        m_ref[...] = jnp.zeros_like(m_ref)

    t_idx = lax.broadcasted_iota(jnp.int32, (L, L), 0)
    s_idx = lax.broadcasted_iota(jnp.int32, (L, L), 1)
    lower = t_idx >= s_idx
    upper = t_idx <= s_idx
    row_l = lax.broadcasted_iota(jnp.int32, (L, 1), 0)

    gate_col = lax.broadcasted_iota(jnp.int32, (L, 4 * n_heads), 1)
    is_forget = (gate_col // n_heads) % 2 == 1

    dirs = (
        (qkf_ref, vf_ref, gf_ref, hf_ref, lower, lower, upper, j == 0),
        (qkb_ref, vb_ref, gb_ref, hb_ref, upper, upper, lower, j == nsteps - 1),
    )
    qk_raw, qc_raw = {}, {}
    for d, (qk_ref, v_ref, g_ref, h_ref, mask, cum_col, cum_row, is_meta) in enumerate(dirs):
        for hd in range(n_heads):
            ch = d * n_heads + hd
            q = qk_ref[:, hd * dk:(hd + 1) * dk]
            k = qk_ref[:, (n_heads + hd) * dk:(n_heads + hd + 1) * dk]
            qk_raw[ch] = lax.dot_general(q, k, (((1,), (1,)), ((), ())),
                                         preferred_element_type=F32)
            qc_raw[ch] = jnp.dot(q, c_ref[ch].astype(BF16), preferred_element_type=F32)

    chains = []
    for d, (qk_ref, v_ref, g_ref, h_ref, mask, cum_col, cum_row, is_meta) in enumerate(dirs):
        valid = jnp.logical_or(jnp.logical_not(is_meta), row_l < n_meta)
        pre = g_ref[...]
        log_sig = jnp.minimum(pre, 0.0) - jnp.log(1.0 + jnp.exp(-jnp.abs(pre)))
        g = jnp.where(valid, jnp.where(is_forget, log_sig, pre), jnp.where(is_forget, 0.0, NEG))
        gt = g.T
        c0 = 2 * d * n_heads
        ig_cols = g[:, c0:c0 + n_heads]
        lf_cols = g[:, c0 + n_heads:c0 + 2 * n_heads]
        ig_rows = gt[c0:c0 + n_heads, :]
        lf_rows = gt[c0 + n_heads:c0 + 2 * n_heads, :]
        b_cols = _split_cumsum(cum_col, lf_cols, True)
        b_rows = _split_cumsum(cum_row, lf_rows, False)
        b_tots = jnp.sum(lf_rows, axis=1, keepdims=True)

        for hd in range(n_heads):
            ch = d * n_heads + hd
            q = qk_ref[:, hd * dk:(hd + 1) * dk]
            k = qk_ref[:, (n_heads + hd) * dk:(n_heads + hd + 1) * dk]
            v = v_ref[:, hd * dv:(hd + 1) * dv]
            v = jnp.where(valid, v, jnp.zeros_like(v))
            b_col = b_cols[:, hd:hd + 1]
            b_row = b_rows[hd:hd + 1, :]
            ig_col = ig_cols[:, hd:hd + 1]
            ig_row = ig_rows[hd:hd + 1, :]
            b_tot = b_tots[hd:hd + 1, :]
            m = m_ref[ch]

            log_d = jnp.where(mask, b_col - b_row + ig_row, NEG)
            log_inter = b_col + m
            m_t = jnp.maximum(log_inter, jnp.max(log_d, axis=1, keepdims=True))
            log_w_row = b_tot - b_row + ig_row
            log_w_col = b_tot - b_col + ig_col
            m_new = jnp.maximum(b_tot + m, jnp.max(log_w_row, axis=1, keepdims=True))
            chains.append(dict(
                ch=ch, q=q, k=k, v=v, m_t=m_t, m_new=m_new, h_ref=h_ref,
                cols=slice(hd * dv, (hd + 1) * dv),
                d_mat=jnp.exp(log_d - m_t),
                inter=jnp.exp(log_inter - m_t),
                decay=jnp.exp(b_tot + m - m_new),
                w_col=jnp.exp(log_w_col - m_new)))

    for c in chains:
        c["qc"] = qc_raw[c["ch"]]
        c["s"] = qk_raw[c["ch"]] * c["d_mat"]
        c["wv"] = (c["w_col"] * c["v"].astype(F32)).astype(BF16)
        qn = jnp.sum(c["q"].astype(F32) * n_ref[c["ch"]], axis=1, keepdims=True)
        den = c["inter"] * qn + jnp.sum(c["s"], axis=1, keepdims=True)
        c["scale"] = 1.0 / jnp.maximum(jnp.abs(den), jnp.exp(-c["m_t"]))
    for c in chains:
        ch, decay = c["ch"], c["decay"]
        sv = jnp.dot(c["s"].astype(BF16), c["v"], preferred_element_type=F32)
        c["h_ref"][:, c["cols"]] = (c["inter"] * c["qc"] + sv) * c["scale"]
        d_c = lax.dot_general(c["k"], c["wv"], (((0,), (0,)), ((), ())),
                              preferred_element_type=F32)
        c_ref[ch] = decay * c_ref[ch] + d_c
        n_ref[ch] = decay * n_ref[ch] + jnp.sum(c["w_col"] * c["k"].astype(F32), axis=0,
                                                keepdims=True)
        m_ref[ch] = c["m_new"]


def _mlstm_scan(qk, v, g, n_tok, n_meta, n_heads, dk, dv):
    t_rows = v.shape[0]
    n_chunks = n_tok // CHUNK + 1
    ng = g.shape[1]

    def fwd(j):
        return (j + n_chunks - 1) % n_chunks

    def bwd(j):
        return (2 * (n_chunks - 1) - j) % n_chunks

    kern = functools.partial(_scan_kernel, n_heads=n_heads, dk=dk, dv=dv, n_meta=n_meta)
    return pl.pallas_call(
        kern,
        grid=(n_chunks,),
        in_specs=[pl.BlockSpec((CHUNK, qk.shape[1]), lambda j: (fwd(j), 0)),
                  pl.BlockSpec((CHUNK, qk.shape[1]), lambda j: (bwd(j), 0)),
                  pl.BlockSpec((CHUNK, v.shape[1]), lambda j: (fwd(j), 0)),
                  pl.BlockSpec((CHUNK, v.shape[1]), lambda j: (bwd(j), 0)),
                  pl.BlockSpec((CHUNK, ng), lambda j: (fwd(j), 0)),
                  pl.BlockSpec((CHUNK, ng), lambda j: (bwd(j), 0))],
        out_specs=[pl.BlockSpec((CHUNK, v.shape[1]), lambda j: (fwd(j), 0)),
                   pl.BlockSpec((CHUNK, v.shape[1]), lambda j: (bwd(j), 0))],
        out_shape=[jax.ShapeDtypeStruct((t_rows, v.shape[1]), F32),
                   jax.ShapeDtypeStruct((t_rows, v.shape[1]), F32)],
        scratch_shapes=[pltpu.VMEM((2 * n_heads, dk, dv), F32),
                        pltpu.VMEM((2 * n_heads, 1, dk), F32),
                        pltpu.VMEM((2 * n_heads, 1, 1), F32)],
        name="mlstm_scan",
        compiler_params=_params("arbitrary"),
    )(qk, qk, v, v, g, g)


def _mlstm_out_kernel(hf_ref, hb_ref, o_ref, g_ref, mix_in_ref, out_ref, *, n_heads, dv):
    del mix_in_ref
    for hd in range(n_heads):
        sl = slice(hd * dv, (hd + 1) * dv)
        h = hf_ref[:, sl] + hb_ref[:, sl]
        ms = jnp.mean(h * h, axis=-1, keepdims=True)
        h = h * lax.rsqrt(ms + RMS_EPS)
        out_ref[:, sl] = (h * g_ref[:, sl] * _sigmoid(o_ref[:, sl])).astype(out_ref.dtype)


def _mlstm_out(hf, hb, o_pre, norm_g, mix, n_heads, dv, tile):
    t_rows, width = hf.shape
    col_blk = (mix.shape[1] - width) // width
    assert col_blk * width == mix.shape[1] - width
    kern = functools.partial(_mlstm_out_kernel, n_heads=n_heads, dv=dv)
    spec = pl.BlockSpec((tile, width), lambda i: (i, 0))
    return pl.pallas_call(
        kern,
        grid=(t_rows // tile,),
        in_specs=[spec, spec, spec, pl.BlockSpec((1, width), lambda i: (0, 0)),
                  pl.BlockSpec(memory_space=pl.ANY)],
        out_specs=pl.BlockSpec((tile, width), lambda i: (i, col_blk)),
        out_shape=jax.ShapeDtypeStruct(mix.shape, mix.dtype),
        input_output_aliases={4: 0},
        name="mlstm_out",
        compiler_params=_params("parallel"),
    )(hf, hb, o_pre, norm_g.reshape(1, width), mix)


def _attn_kernel(q_ref, k_ref, v_ref, km_ref, vm_ref, bias_ref, mb_ref, o_ref, *, n_rows, hd_dim):
    blk = pl.program_id(1)
    scale = hd_dim ** -0.5 * LOG2E
    nt = (((1,), (1,)), ((), ()))

    def meta_part(hh):
        cols = slice(hh * hd_dim, (hh + 1) * hd_dim)
        sm = lax.dot_general(q_ref[:, cols], km_ref[:, cols], nt,
                             preferred_element_type=F32) * scale + mb_ref[hh]
        m_meta = jnp.max(sm, axis=1, keepdims=True)
        pm = jnp.exp2(sm - m_meta)
        l_meta = jnp.sum(pm, axis=1, keepdims=True)
        o_meta = jnp.dot(pm.astype(BF16), vm_ref[:, cols], preferred_element_type=F32)
        return m_meta, l_meta, o_meta

    units = [(rr, hh) for rr in range(ROWS_PER_STEP) for hh in range(HEADS_PER_STEP)]

    def window_start(rr):
        r = blk * ROWS_PER_STEP + rr
        rs = jnp.clip(r - WIN_ROWS // 2, 0, n_rows - WIN_ROWS)
        return pl.multiple_of(rs * GRID_W, GRID_W), rs - r + WIN_ROWS - 1

    def scores(rr, hh):
        cols = slice(hh * hd_dim, (hh + 1) * hd_dim)
        k0, variant = window_start(rr)
        q = q_ref[rr * GRID_W:(rr + 1) * GRID_W, cols]
        kw = k_ref[pl.ds(k0, WIN_ROWS * GRID_W), cols]
        return lax.dot_general(q, kw, nt, preferred_element_type=F32) * scale + bias_ref[variant, hh]

    def finish(rr, hh, s):
        cols = slice(hh * hd_dim, (hh + 1) * hd_dim)
        rows = slice(rr * GRID_W, (rr + 1) * GRID_W)
        k0, _ = window_start(rr)
        m_meta, l_meta, o_meta = meta[hh]
        m_loc = jnp.max(s, axis=1, keepdims=True)
        p = jnp.exp2(s - m_loc)
        l_loc = jnp.sum(p, axis=1, keepdims=True)
        vw = v_ref[pl.ds(k0, WIN_ROWS * GRID_W), cols]
        o_loc = jnp.dot(p.astype(BF16), vw, preferred_element_type=F32)
        m_all = jnp.maximum(m_loc, m_meta[rows])
        a_loc = jnp.exp2(m_loc - m_all)
        a_meta = jnp.exp2(m_meta[rows] - m_all)
        o = (a_loc * o_loc + a_meta * o_meta[rows]) / (a_loc * l_loc + a_meta * l_meta[rows])
        o_ref[rows, cols] = o.astype(o_ref.dtype)

    ahead = 4
    pending = [scores(*units[u]) for u in range(ahead)]
    meta = [meta_part(hh) for hh in range(HEADS_PER_STEP)]
    for u, (rr, hh) in enumerate(units):
        if u + ahead < len(units):
            pending.append(scores(*units[u + ahead]))
        finish(rr, hh, pending.pop(0))


def _attn_meta_kernel(q_ref, k_ref, v_ref, mb_ref, att_in_ref, o_ref, *, n_heads, hd_dim):
    del att_in_ref
    scale = hd_dim ** -0.5
    for h in range(n_heads):
        cols = slice(h * hd_dim, (h + 1) * hd_dim)
        s = lax.dot_general(q_ref[:, cols], k_ref[:, cols], (((1,), (1,)), ((), ())),
                            preferred_element_type=F32) * scale + mb_ref[h]
        p = jnp.exp(s - jnp.max(s, axis=1, keepdims=True))
        denom = jnp.sum(p, axis=1, keepdims=True)
        o = jnp.dot(p.astype(BF16), v_ref[:, cols], preferred_element_type=F32)
        o_ref[:, cols] = (o / denom).astype(o_ref.dtype)


def _attention_bias(rel_bias):
    depth, n_heads = rel_bias.shape[:2]
    c = jnp.arange(GRID_W)
    cs = jnp.clip(c - WIN_COLS // 2, 0, GRID_W - WIN_COLS)
    in_win = (c[None, :] >= cs[:, None]) & (c[None, :] < cs[:, None] + WIN_COLS)
    rb = jnp.pad(rel_bias.astype(F32) * LOG2E, ((0, 0),) * 3 + ((GRID_W, GRID_W),))
    e = jnp.stack([rb[..., GRID_W + WIN_COLS - 1 - q:2 * GRID_W + WIN_COLS - 1 - q]
                   for q in range(GRID_W)], axis=3)
    e = jnp.where(in_win, e, NEG)
    bv = jnp.stack([e[:, :, v:v + WIN_ROWS] for v in range(WIN_ROWS)], axis=1)
    bv = bv.transpose(0, 1, 2, 4, 3, 5)
    return bv.reshape(depth, WIN_ROWS, n_heads, GRID_W, WIN_ROWS * GRID_W)


def _attention(qkv, bias, layer, meta_bias, n_tok, n_meta, n_heads, hd_dim, out_width):
    t_rows = qkv.shape[0]
    n_rows = n_tok // GRID_W
    width = n_heads * hd_dim
    gw = HEADS_PER_STEP * hd_dim
    n_groups = n_heads // HEADS_PER_STEP
    q_tile = ROWS_PER_STEP * GRID_W
    mb = meta_bias.astype(F32).reshape(n_heads, 1, n_meta)
    meta_blk = n_tok // n_meta

    kern = functools.partial(_attn_kernel, n_rows=n_rows, hd_dim=hd_dim)
    att = pl.pallas_call(
        kern,
        grid=(n_groups, n_rows // ROWS_PER_STEP),
        in_specs=[pl.BlockSpec((q_tile, gw), lambda g, b: (b, g)),
                  pl.BlockSpec((n_tok, gw), lambda g, b: (0, n_groups + g)),
                  pl.BlockSpec((n_tok, gw), lambda g, b: (0, 2 * n_groups + g)),
                  pl.BlockSpec((n_meta, gw), lambda g, b: (meta_blk, n_groups + g)),
                  pl.BlockSpec((n_meta, gw), lambda g, b: (meta_blk, 2 * n_groups + g)),
                  pl.BlockSpec((None, WIN_ROWS, HEADS_PER_STEP, GRID_W, WIN_ROWS * GRID_W),
                               lambda g, b: (layer, 0, g, 0, 0)),
                  pl.BlockSpec((HEADS_PER_STEP, 1, n_meta), lambda g, b: (g, 0, 0))],
        out_specs=pl.BlockSpec((q_tile, gw), lambda g, b: (b, g)),
        out_shape=jax.ShapeDtypeStruct((t_rows, out_width), BF16),
        name="attn",
        compiler_params=_params("parallel", "arbitrary"),
    )(qkv, qkv, qkv, qkv, qkv, bias, mb * LOG2E)

    meta_kern = functools.partial(_attn_meta_kernel, n_heads=n_heads, hd_dim=hd_dim)
    return pl.pallas_call(
        meta_kern,
        grid=(1,),
        in_specs=[pl.BlockSpec((n_meta, width), lambda i: (meta_blk, 0)),
                  pl.BlockSpec((n_meta, width), lambda i: (meta_blk, 1)),
                  pl.BlockSpec((n_meta, width), lambda i: (meta_blk, 2)),
                  pl.BlockSpec((n_heads, 1, n_meta), lambda i: (0, 0, 0)),
                  pl.BlockSpec(memory_space=pl.ANY)],
        out_specs=pl.BlockSpec((n_meta, width), lambda i: (meta_blk, 0)),
        out_shape=jax.ShapeDtypeStruct((t_rows, out_width), BF16),
        input_output_aliases={4: 0},
        name="attn_meta",
        compiler_params=_params("arbitrary"),
    )(qkv, qkv, qkv, mb, att)


def kernel(x, meta_tokens, norm_mix_g, w_in, gate_bias, conv_w, conv_b, rel_bias, meta_bias,
           mlstm_norm_g, w_out, norm_ffn_g, w_gate, w_up, w_down, final_norm_g):
    batch, n_tok, d_model = x.shape
    assert batch == 1, "written for a single sequence"
    n_meta = meta_tokens.shape[0]
    depth = w_in.shape[0]
    t_rows = n_tok + n_meta
    att_heads = rel_bias.shape[1]
    n_gates = gate_bias.shape[1]
    ml_heads = n_gates // 4
    qk_width = conv_w.shape[2] // 2
    v_width = mlstm_norm_g.shape[1]
    att_width = (w_in.shape[2] - 2 * qk_width - 2 * v_width - n_gates) // 3
    hd_dim = att_width // att_heads
    dk = qk_width // ml_heads
    dv = v_width // ml_heads
    assert n_tok % (GRID_W * ROWS_PER_STEP) == 0 and n_meta % 16 == 0 and n_tok % n_meta == 0
    assert att_heads % HEADS_PER_STEP == 0

    tm = _divisor_tile(t_rows, 704, 16)
    tm_big = _divisor_tile(t_rows, 3328, 16)
    tr = tm
    conv_tile = _divisor_tile(n_tok, 512, CHUNK)

    seg_widths = (3 * att_width, 2 * qk_width, v_width, v_width)
    seg_dtypes = (BF16, F32, BF16, F32)
    q_scale = jnp.concatenate([jnp.full((qk_width,), dk ** -0.5, F32), jnp.ones((qk_width,), F32)])

    w_in_b = w_in.astype(BF16)
    w_out_b = w_out.astype(BF16)
    w_down_b = w_down.astype(BF16)
    attn_bias = _attention_bias(rel_bias)

    h = jnp.concatenate([x[0], meta_tokens.astype(x.dtype)], axis=0)
    for l in range(depth):
        u = _rmsnorm(h, norm_mix_g[l], t_rows, BF16, tr)
        qkv, mqk, mv, mo, g = _in_proj(u, w_in_b, gate_bias[l], l, seg_widths, seg_dtypes, tm)

        mix = _attention(qkv, attn_bias, l, meta_bias[l], n_tok, n_meta, att_heads, hd_dim,
                         att_width + v_width)
        qk = _qk_conv(mqk, conv_w[l], conv_b[l], q_scale, n_tok, n_meta, conv_tile)
        hf, hb = _mlstm_scan(qk, mv, g, n_tok, n_meta, ml_heads, dk, dv)
        mix = _mlstm_out(hf, hb, mo, mlstm_norm_g[l], mix, ml_heads, dv, tr)

        h = _matmul_residual(mix, w_out_b, l, h, tm, 1024, "out_proj")
        z = _rmsnorm(h, norm_ffn_g[l], t_rows, BF16, tr)
        act = _swiglu(z, w_gate, w_up, l, tm_big, 256)
        h = _matmul_residual(act, w_down_b, l, h, tm, 256, "down_proj")
    y = _rmsnorm(h, final_norm_g, n_tok, x.dtype, _divisor_tile(n_tok, 512, 16))
    return y[None]
```

```python
import functools

import jax
import jax.numpy as jnp
from jax import lax
from jax.experimental import pallas as pl
from jax.experimental.pallas import tpu as pltpu

F32 = jnp.float32
BF16 = jnp.bfloat16

GRID_W = 64
WIN_ROWS = 8
WIN_COLS = 16
CHUNK = 64
RMS_EPS = 1e-6
LANES = 128
LOG2E = 1.4426950408889634
NEG = -1e30
ROWS_PER_STEP = 32
HEADS_PER_STEP = 2

V7X_VMEM_BYTES = 64 * 1024 * 1024
V7X_VMEM_LIMIT = V7X_VMEM_BYTES - 8 * 1024 * 1024
V7X_VMEM_LIMIT_MAX = V7X_VMEM_BYTES - 3 * 1024 * 1024


def _params(*sem, vmem_limit=V7X_VMEM_LIMIT):
    return pltpu.CompilerParams(dimension_semantics=sem, vmem_limit_bytes=vmem_limit)


def _divisor_tile(n, target, mult):
    best = None
    for d in range(mult, min(n, target) + 1, mult):
        if n % d == 0:
            best = d
    return best if best is not None else n


def _sigmoid(x):
    return 1.0 / (1.0 + jnp.exp(-x))


def _rmsnorm_kernel(x_ref, g_ref, o_ref):
    x = x_ref[...]
    ms = jnp.mean(x * x, axis=-1, keepdims=True)
    o_ref[...] = (x * lax.rsqrt(ms + RMS_EPS) * g_ref[...]).astype(o_ref.dtype)


def _rmsnorm(x, g, out_rows, out_dtype, tile):
    d = x.shape[1]
    return pl.pallas_call(
        _rmsnorm_kernel,
        grid=(out_rows // tile,),
        in_specs=[pl.BlockSpec((tile, d), lambda i: (i, 0)),
                  pl.BlockSpec((1, d), lambda i: (0, 0))],
        out_specs=pl.BlockSpec((tile, d), lambda i: (i, 0)),
        out_shape=jax.ShapeDtypeStruct((out_rows, d), out_dtype),
        name="rmsnorm",
        compiler_params=_params("parallel"),
    )(x, g.reshape(1, d))


def _in_proj_kernel(u_ref, w_ref, wg_ref, bg_ref, *out_refs, bounds):
    j = pl.program_id(1)
    seg_refs, g_ref = out_refs[:-1], out_refs[-1]
    for s, o_ref in enumerate(seg_refs):
        @pl.when(jnp.logical_and(j >= bounds[s], j < bounds[s + 1]))
        def _(o_ref=o_ref):
            o_ref[...] = jnp.dot(u_ref[...], w_ref[...], preferred_element_type=F32).astype(o_ref.dtype)

    @pl.when(j == 0)
    def _():
        ng = g_ref.shape[1]
        g_ref[...] = jnp.dot(u_ref[...], wg_ref[:, :ng], preferred_element_type=F32) + bg_ref[...]


def _in_proj(u, w, gate_bias, layer, seg_widths, seg_dtypes, tm):
    m, k = u.shape
    ng = gate_bias.shape[0]
    gate_col0 = sum(seg_widths)
    assert gate_col0 % LANES == 0 and ng <= LANES and gate_col0 + ng == w.shape[2]
    tn = next(t for t in (1024, 512, 256, 128) if all(sw % t == 0 for sw in seg_widths))
    bounds = [0]
    for sw in seg_widths:
        bounds.append(bounds[-1] + sw // tn)

    def seg_map(s):
        return lambda i, j: (i, jnp.clip(j - bounds[s], 0, bounds[s + 1] - bounds[s] - 1))

    kern = functools.partial(_in_proj_kernel, bounds=tuple(bounds))
    return pl.pallas_call(
        kern,
        grid=(m // tm, bounds[-1]),
        in_specs=[pl.BlockSpec((tm, k), lambda i, j: (i, 0)),
                  pl.BlockSpec((None, k, tn), lambda i, j: (layer, 0, j)),
                  pl.BlockSpec((None, k, LANES), lambda i, j: (layer, 0, gate_col0 // LANES)),
                  pl.BlockSpec((1, ng), lambda i, j: (0, 0))],
        out_specs=[pl.BlockSpec((tm, tn), seg_map(s)) for s in range(len(seg_widths))]
        + [pl.BlockSpec((tm, ng), lambda i, j: (i, 0))],
        out_shape=[jax.ShapeDtypeStruct((m, sw), dt) for sw, dt in zip(seg_widths, seg_dtypes)]
        + [jax.ShapeDtypeStruct((m, ng), F32)],
        name="in_proj",
        compiler_params=_params("parallel", "arbitrary"),
    )(u, w, w, gate_bias.reshape(1, ng))


def _mm_res_kernel(a_ref, w_ref, r_ref, o_ref):
    o_ref[...] = r_ref[...] + jnp.dot(a_ref[...], w_ref[...], preferred_element_type=F32)


def _matmul_residual(a, w, layer, res, tm, tn, name):
    m, k = a.shape
    n = w.shape[2]
    tn = min(tn, n)
    tile_bytes = 2 * (tm * k * a.dtype.itemsize + k * tn * w.dtype.itemsize + 2 * tm * tn * 4)
    vmem_limit = V7X_VMEM_LIMIT if tile_bytes < V7X_VMEM_LIMIT - (4 << 20) else V7X_VMEM_LIMIT_MAX
    return pl.pallas_call(
        _mm_res_kernel,
        grid=(m // tm, pl.cdiv(n, tn)),
        in_specs=[pl.BlockSpec((tm, k), lambda i, j: (i, 0)),
                  pl.BlockSpec((None, k, tn), lambda i, j: (layer, 0, j)),
                  pl.BlockSpec((tm, tn), lambda i, j: (i, j))],
        out_specs=pl.BlockSpec((tm, tn), lambda i, j: (i, j)),
        out_shape=jax.ShapeDtypeStruct((m, n), F32),
        name=name,
        compiler_params=_params("parallel", "arbitrary", vmem_limit=vmem_limit),
    )(a, w, res)


def _swiglu_kernel(a_ref, wg_ref, wu_ref, o_ref):
    a = a_ref[...]
    g = jnp.dot(a, wg_ref[...].astype(a.dtype), preferred_element_type=F32)
    u = jnp.dot(a, wu_ref[...].astype(a.dtype), preferred_element_type=F32)
    o_ref[...] = (g * _sigmoid(g) * u).astype(o_ref.dtype)


def _swiglu(a, wg, wu, layer, tm, tn):
    m, k = a.shape
    n = wg.shape[2]
    tn = min(tn, n)
    return pl.pallas_call(
        _swiglu_kernel,
        grid=(m // tm, pl.cdiv(n, tn)),
        in_specs=[pl.BlockSpec((tm, k), lambda i, j: (i, 0), pipeline_mode=pl.Buffered(1)),
                  pl.BlockSpec((None, k, tn), lambda i, j: (layer, 0, j)),
                  pl.BlockSpec((None, k, tn), lambda i, j: (layer, 0, j))],
        out_specs=pl.BlockSpec((tm, tn), lambda i, j: (i, j)),
        out_shape=jax.ShapeDtypeStruct((m, n), BF16),
        name="swiglu",
        compiler_params=_params("parallel", "arbitrary"),
    )(a, wg, wu)


def _conv_kernel(x_ref, prev_ref, next_ref, w_ref, b_ref, s_ref, o_ref, ext_ref, *, n_meta, half):
    i = pl.program_id(0)
    n = pl.num_programs(0)
    tile = x_ref.shape[0]
    is_meta = i == n - 1
    row = lax.broadcasted_iota(jnp.int32, x_ref.shape, 0)
    x = jnp.where(jnp.logical_and(is_meta, row >= n_meta), 0.0, x_ref[...])
    ext_ref[0:8, :] = jnp.where(is_meta, 0.0, prev_ref[...])
    ext_ref[8:8 + tile, :] = x
    ext_ref[8 + tile:16 + tile, :] = jnp.where(i == n - 2, 0.0, next_ref[...])

    width = 2 * half + 1

    def conv_rows(start, rows):
        y = b_ref[...] + w_ref[0:1, :] * ext_ref[pl.ds(8 + start - half, rows), :]
        for j in range(1, width):
            y = y + w_ref[j:j + 1, :] * ext_ref[pl.ds(8 + start - half + j, rows), :]
        return y

    def finish(y):
        return (y * _sigmoid(y) * s_ref[...]).astype(o_ref.dtype)

    out = finish(conv_rows(0, tile))
    o_ref[...] = jnp.where(jnp.logical_and(is_meta, row >= n_meta), jnp.zeros_like(out), out)

    @pl.when(is_meta)
    def _():
        base = n_meta - 8
        y = conv_rows(base, 8)
        r8 = lax.broadcasted_iota(jnp.int32, y.shape, 0) + base
        for t in range(half):
            for j in range(half + 1 + t, width):
                src = j - half - 1 - t
                y = y + jnp.where(r8 == n_meta - 1 - t,
                                  w_ref[j:j + 1, :] * next_ref[src:src + 1, :], 0.0)
        o_ref[base:base + 8, :] = finish(y)


def _qk_conv(x, w, b, scale, n_tok, n_meta, tile):
    c = x.shape[1]
    n_grid_tiles = n_tok // tile
    kw = w.shape[0]
    half = kw // 2
    t8 = tile // 8
    last8 = (n_tok + n_meta) // 8 - 1

    def prev_map(i):
        return (jnp.where(i == 0, last8, jnp.minimum(i, n_grid_tiles) * t8 - 1), 0)

    def next_map(i):
        return (jnp.where(i >= n_grid_tiles - 1, 0, (i + 1) * t8), 0)

    kern = functools.partial(_conv_kernel, n_meta=n_meta, half=half)
    return pl.pallas_call(
        kern,
        grid=(n_grid_tiles + 1,),
        in_specs=[pl.BlockSpec((tile, c), lambda i: (i, 0)),
                  pl.BlockSpec((8, c), prev_map),
                  pl.BlockSpec((8, c), next_map),
                  pl.BlockSpec((kw, c), lambda i: (0, 0)),
                  pl.BlockSpec((1, c), lambda i: (0, 0)),
                  pl.BlockSpec((1, c), lambda i: (0, 0))],
        out_specs=pl.BlockSpec((tile, c), lambda i: (i, 0)),
        out_shape=jax.ShapeDtypeStruct((n_tok + CHUNK, c), BF16),
        scratch_shapes=[pltpu.VMEM((tile + 16, c), F32)],
        name="mlstm_qk_conv",
        compiler_params=_params("parallel"),
    )(x, x, x, w, b.reshape(1, c), scale.reshape(1, c))


def _split_cumsum(mask, x, mask_first):
    m = mask.astype(BF16)
    total = None
    rest = x
    for _ in range(3):
        part = rest.astype(BF16)
        rest = rest - part.astype(F32)
        term = (jnp.dot(m, part, preferred_element_type=F32) if mask_first
                else jnp.dot(part, m, preferred_element_type=F32))
        total = term if total is None else total + term
    return total


def _scan_kernel(qkf_ref, qkb_ref, vf_ref, vb_ref, gf_ref, gb_ref,
                 hf_ref, hb_ref, c_ref, n_ref, m_ref, *, n_heads, dk, dv, n_meta):
    j = pl.program_id(0)
    nsteps = pl.num_programs(0)
    L = CHUNK

    @pl.when(j == 0)
    def _():
        c_ref[...] = jnp.zeros_like(c_ref)
        n_ref[...] = jnp.zeros_like(n_ref)
        m_ref[...] = jnp.zeros_like(m_ref)

    t_idx = lax.broadcasted_iota(jnp.int32, (L, L), 0)
    s_idx = lax.broadcasted_iota(jnp.int32, (L, L), 1)
    lower = t_idx >= s_idx
    upper = t_idx <= s_idx
    row_l = lax.broadcasted_iota(jnp.int32, (L, 1), 0)

    gate_col = lax.broadcasted_iota(jnp.int32, (L, 4 * n_heads), 1)
    is_forget = (gate_col // n_heads) % 2 == 1

    dirs = (
        (qkf_ref, vf_ref, gf_ref, hf_ref, lower, lower, upper, j == 0),
        (qkb_ref, vb_ref, gb_ref, hb_ref, upper, upper, lower, j == nsteps - 1),
    )
    chains = []
    for d, (qk_ref, v_ref, g_ref, h_ref, mask, cum_col, cum_row, is_meta) in enumerate(dirs):
        valid = jnp.logical_or(jnp.logical_not(is_meta), row_l < n_meta)
        pre = g_ref[...]
        log_sig = jnp.minimum(pre, 0.0) - jnp.log(1.0 + jnp.exp(-jnp.abs(pre)))
        g = jnp.where(valid, jnp.where(is_forget, log_sig, pre), jnp.where(is_forget, 0.0, NEG))
        gt = g.T
        c0 = 2 * d * n_heads
        ig_cols = g[:, c0:c0 + n_heads]
        lf_cols = g[:, c0 + n_heads:c0 + 2 * n_heads]
        ig_rows = gt[c0:c0 + n_heads, :]
        lf_rows = gt[c0 + n_heads:c0 + 2 * n_heads, :]
        b_cols = _split_cumsum(cum_col, lf_cols, True)
        b_rows = _split_cumsum(cum_row, lf_rows, False)
        b_tots = jnp.sum(lf_rows, axis=1, keepdims=True)

        for hd in range(n_heads):
            ch = d * n_heads + hd
            q = qk_ref[:, hd * dk:(hd + 1) * dk]
            k = qk_ref[:, (n_heads + hd) * dk:(n_heads + hd + 1) * dk]
            v = v_ref[:, hd * dv:(hd + 1) * dv]
            v = jnp.where(valid, v, jnp.zeros_like(v))
            b_col = b_cols[:, hd:hd + 1]
            b_row = b_rows[hd:hd + 1, :]
            ig_col = ig_cols[:, hd:hd + 1]
            ig_row = ig_rows[hd:hd + 1, :]
            b_tot = b_tots[hd:hd + 1, :]
            m = m_ref[ch]

            log_d = jnp.where(mask, b_col - b_row + ig_row, NEG)
            log_inter = b_col + m
            m_t = jnp.maximum(log_inter, jnp.max(log_d, axis=1, keepdims=True))
            log_w_row = b_tot - b_row + ig_row
            log_w_col = b_tot - b_col + ig_col
            m_new = jnp.maximum(b_tot + m, jnp.max(log_w_row, axis=1, keepdims=True))
            chains.append(dict(
                ch=ch, q=q, k=k, v=v, m_t=m_t, m_new=m_new, h_ref=h_ref,
                cols=slice(hd * dv, (hd + 1) * dv),
                d_mat=jnp.exp(log_d - m_t),
                inter=jnp.exp(log_inter - m_t),
                decay=jnp.exp(b_tot + m - m_new),
                w_col=jnp.exp(log_w_col - m_new)))

    for c in chains:
        c["qk"] = lax.dot_general(c["q"], c["k"], (((1,), (1,)), ((), ())),
                                  preferred_element_type=F32)
        c["qc"] = jnp.dot(c["q"], c_ref[c["ch"]].astype(BF16), preferred_element_type=F32)
    for c in chains:
        c["s"] = c["qk"] * c["d_mat"]
        c["wv"] = (c["w_col"] * c["v"].astype(F32)).astype(BF16)
        qn = jnp.sum(c["q"].astype(F32) * n_ref[c["ch"]], axis=1, keepdims=True)
        den = c["inter"] * qn + jnp.sum(c["s"], axis=1, keepdims=True)
        c["scale"] = 1.0 / jnp.maximum(jnp.abs(den), jnp.exp(-c["m_t"]))
    for c in chains:
        ch, decay = c["ch"], c["decay"]
        sv = jnp.dot(c["s"].astype(BF16), c["v"], preferred_element_type=F32)
        c["h_ref"][:, c["cols"]] = (c["inter"] * c["qc"] + sv) * c["scale"]
        d_c = lax.dot_general(c["k"], c["wv"], (((0,), (0,)), ((), ())),
                              preferred_element_type=F32)
        c_ref[ch] = decay * c_ref[ch] + d_c
        n_ref[ch] = decay * n_ref[ch] + jnp.sum(c["w_col"] * c["k"].astype(F32), axis=0,
                                                keepdims=True)
        m_ref[ch] = c["m_new"]


def _mlstm_scan(qk, v, g, n_tok, n_meta, n_heads, dk, dv):
    t_rows = v.shape[0]
    n_chunks = n_tok // CHUNK + 1
    ng = g.shape[1]

    def fwd(j):
        return (j + n_chunks - 1) % n_chunks

    def bwd(j):
        return (2 * (n_chunks - 1) - j) % n_chunks

    kern = functools.partial(_scan_kernel, n_heads=n_heads, dk=dk, dv=dv, n_meta=n_meta)
    return pl.pallas_call(
        kern,
        grid=(n_chunks,),
        in_specs=[pl.BlockSpec((CHUNK, qk.shape[1]), lambda j: (fwd(j), 0)),
                  pl.BlockSpec((CHUNK, qk.shape[1]), lambda j: (bwd(j), 0)),
                  pl.BlockSpec((CHUNK, v.shape[1]), lambda j: (fwd(j), 0)),
                  pl.BlockSpec((CHUNK, v.shape[1]), lambda j: (bwd(j), 0)),
                  pl.BlockSpec((CHUNK, ng), lambda j: (fwd(j), 0)),
                  pl.BlockSpec((CHUNK, ng), lambda j: (bwd(j), 0))],
        out_specs=[pl.BlockSpec((CHUNK, v.shape[1]), lambda j: (fwd(j), 0)),
                   pl.BlockSpec((CHUNK, v.shape[1]), lambda j: (bwd(j), 0))],
        out_shape=[jax.ShapeDtypeStruct((t_rows, v.shape[1]), F32),
                   jax.ShapeDtypeStruct((t_rows, v.shape[1]), F32)],
        scratch_shapes=[pltpu.VMEM((2 * n_heads, dk, dv), F32),
                        pltpu.VMEM((2 * n_heads, 1, dk), F32),
                        pltpu.VMEM((2 * n_heads, 1, 1), F32)],
        name="mlstm_scan",
        compiler_params=_params("arbitrary"),
    )(qk, qk, v, v, g, g)


def _mlstm_out_kernel(hf_ref, hb_ref, o_ref, g_ref, mix_in_ref, out_ref, *, n_heads, dv):
    del mix_in_ref
    for hd in range(n_heads):
        sl = slice(hd * dv, (hd + 1) * dv)
        h = hf_ref[:, sl] + hb_ref[:, sl]
        ms = jnp.mean(h * h, axis=-1, keepdims=True)
        h = h * lax.rsqrt(ms + RMS_EPS)
        out_ref[:, sl] = (h * g_ref[:, sl] * _sigmoid(o_ref[:, sl])).astype(out_ref.dtype)


def _mlstm_out(hf, hb, o_pre, norm_g, mix, n_heads, dv, tile):
    t_rows, width = hf.shape
    col_blk = (mix.shape[1] - width) // width
    assert col_blk * width == mix.shape[1] - width
    kern = functools.partial(_mlstm_out_kernel, n_heads=n_heads, dv=dv)
    spec = pl.BlockSpec((tile, width), lambda i: (i, 0))
    return pl.pallas_call(
        kern,
        grid=(t_rows // tile,),
        in_specs=[spec, spec, spec, pl.BlockSpec((1, width), lambda i: (0, 0)),
                  pl.BlockSpec(memory_space=pl.ANY)],
        out_specs=pl.BlockSpec((tile, width), lambda i: (i, col_blk)),
        out_shape=jax.ShapeDtypeStruct(mix.shape, mix.dtype),
        input_output_aliases={4: 0},
        name="mlstm_out",
        compiler_params=_params("parallel"),
    )(hf, hb, o_pre, norm_g.reshape(1, width), mix)


def _attn_kernel(q_ref, k_ref, v_ref, km_ref, vm_ref, bias_ref, mb_ref, o_ref, *, n_rows, hd_dim):
    blk = pl.program_id(1)
    scale = hd_dim ** -0.5 * LOG2E
    nt = (((1,), (1,)), ((), ()))

    def meta_part(hh):
        cols = slice(hh * hd_dim, (hh + 1) * hd_dim)
        sm = lax.dot_general(q_ref[:, cols], km_ref[:, cols], nt,
                             preferred_element_type=F32) * scale + mb_ref[hh]
        m_meta = jnp.max(sm, axis=1, keepdims=True)
        pm = jnp.exp2(sm - m_meta)
        l_meta = jnp.sum(pm, axis=1, keepdims=True)
        o_meta = jnp.dot(pm.astype(BF16), vm_ref[:, cols], preferred_element_type=F32)
        return m_meta, l_meta, o_meta

    units = [(rr, hh) for rr in range(ROWS_PER_STEP) for hh in range(HEADS_PER_STEP)]

    def window_start(rr):
        r = blk * ROWS_PER_STEP + rr
        rs = jnp.clip(r - WIN_ROWS // 2, 0, n_rows - WIN_ROWS)
        return pl.multiple_of(rs * GRID_W, GRID_W), rs - r + WIN_ROWS - 1

    def scores(rr, hh):
        cols = slice(hh * hd_dim, (hh + 1) * hd_dim)
        k0, variant = window_start(rr)
        q = q_ref[rr * GRID_W:(rr + 1) * GRID_W, cols]
        kw = k_ref[pl.ds(k0, WIN_ROWS * GRID_W), cols]
        return lax.dot_general(q, kw, nt, preferred_element_type=F32) * scale + bias_ref[variant, hh]

    def finish(rr, hh, s):
        cols = slice(hh * hd_dim, (hh + 1) * hd_dim)
        rows = slice(rr * GRID_W, (rr + 1) * GRID_W)
        k0, _ = window_start(rr)
        m_meta, l_meta, o_meta = meta[hh]
        m_loc = jnp.max(s, axis=1, keepdims=True)
        p = jnp.exp2(s - m_loc)
        l_loc = jnp.sum(p, axis=1, keepdims=True)
        vw = v_ref[pl.ds(k0, WIN_ROWS * GRID_W), cols]
        o_loc = jnp.dot(p.astype(BF16), vw, preferred_element_type=F32)
        m_all = jnp.maximum(m_loc, m_meta[rows])
        a_loc = jnp.exp2(m_loc - m_all)
        a_meta = jnp.exp2(m_meta[rows] - m_all)
        o = (a_loc * o_loc + a_meta * o_meta[rows]) / (a_loc * l_loc + a_meta * l_meta[rows])
        o_ref[rows, cols] = o.astype(o_ref.dtype)

    ahead = 4
    pending = [scores(*units[u]) for u in range(ahead)]
    meta = [meta_part(hh) for hh in range(HEADS_PER_STEP)]
    for u, (rr, hh) in enumerate(units):
        if u + ahead < len(units):
            pending.append(scores(*units[u + ahead]))
        finish(rr, hh, pending.pop(0))


def _attn_meta_kernel(q_ref, k_ref, v_ref, mb_ref, att_in_ref, o_ref, *, n_heads, hd_dim):
    del att_in_ref
    scale = hd_dim ** -0.5
    for h in range(n_heads):
        cols = slice(h * hd_dim, (h + 1) * hd_dim)
        s = lax.dot_general(q_ref[:, cols], k_ref[:, cols], (((1,), (1,)), ((), ())),
                            preferred_element_type=F32) * scale + mb_ref[h]
        p = jnp.exp(s - jnp.max(s, axis=1, keepdims=True))
        denom = jnp.sum(p, axis=1, keepdims=True)
        o = jnp.dot(p.astype(BF16), v_ref[:, cols], preferred_element_type=F32)
        o_ref[:, cols] = (o / denom).astype(o_ref.dtype)


def _attention_bias(rel_bias):
    depth, n_heads = rel_bias.shape[:2]
    c = jnp.arange(GRID_W)
    cs = jnp.clip(c - WIN_COLS // 2, 0, GRID_W - WIN_COLS)
    in_win = (c[None, :] >= cs[:, None]) & (c[None, :] < cs[:, None] + WIN_COLS)
    rb = jnp.pad(rel_bias.astype(F32) * LOG2E, ((0, 0),) * 3 + ((GRID_W, GRID_W),))
    e = jnp.stack([rb[..., GRID_W + WIN_COLS - 1 - q:2 * GRID_W + WIN_COLS - 1 - q]
                   for q in range(GRID_W)], axis=3)
    e = jnp.where(in_win, e, NEG)
    bv = jnp.stack([e[:, :, v:v + WIN_ROWS] for v in range(WIN_ROWS)], axis=1)
    bv = bv.transpose(0, 1, 2, 4, 3, 5)
    return bv.reshape(depth, WIN_ROWS, n_heads, GRID_W, WIN_ROWS * GRID_W)


def _attention(qkv, bias, layer, meta_bias, n_tok, n_meta, n_heads, hd_dim, out_width):
    t_rows = qkv.shape[0]
    n_rows = n_tok // GRID_W
    width = n_heads * hd_dim
    gw = HEADS_PER_STEP * hd_dim
    n_groups = n_heads // HEADS_PER_STEP
    q_tile = ROWS_PER_STEP * GRID_W
    mb = meta_bias.astype(F32).reshape(n_heads, 1, n_meta)
    meta_blk = n_tok // n_meta

    kern = functools.partial(_attn_kernel, n_rows=n_rows, hd_dim=hd_dim)
    att = pl.pallas_call(
        kern,
        grid=(n_groups, n_rows // ROWS_PER_STEP),
        in_specs=[pl.BlockSpec((q_tile, gw), lambda g, b: (b, g)),
                  pl.BlockSpec((n_tok, gw), lambda g, b: (0, n_groups + g)),
                  pl.BlockSpec((n_tok, gw), lambda g, b: (0, 2 * n_groups + g)),
                  pl.BlockSpec((n_meta, gw), lambda g, b: (meta_blk, n_groups + g)),
                  pl.BlockSpec((n_meta, gw), lambda g, b: (meta_blk, 2 * n_groups + g)),
                  pl.BlockSpec((None, WIN_ROWS, HEADS_PER_STEP, GRID_W, WIN_ROWS * GRID_W),
                               lambda g, b: (layer, 0, g, 0, 0)),
                  pl.BlockSpec((HEADS_PER_STEP, 1, n_meta), lambda g, b: (g, 0, 0))],
        out_specs=pl.BlockSpec((q_tile, gw), lambda g, b: (b, g)),
        out_shape=jax.ShapeDtypeStruct((t_rows, out_width), BF16),
        name="attn",
        compiler_params=_params("parallel", "arbitrary"),
    )(qkv, qkv, qkv, qkv, qkv, bias, mb * LOG2E)

    meta_kern = functools.partial(_attn_meta_kernel, n_heads=n_heads, hd_dim=hd_dim)
    return pl.pallas_call(
        meta_kern,
        grid=(1,),
        in_specs=[pl.BlockSpec((n_meta, width), lambda i: (meta_blk, 0)),
                  pl.BlockSpec((n_meta, width), lambda i: (meta_blk, 1)),
                  pl.BlockSpec((n_meta, width), lambda i: (meta_blk, 2)),
                  pl.BlockSpec((n_heads, 1, n_meta), lambda i: (0, 0, 0)),
                  pl.BlockSpec(memory_space=pl.ANY)],
        out_specs=pl.BlockSpec((n_meta, width), lambda i: (meta_blk, 0)),
        out_shape=jax.ShapeDtypeStruct((t_rows, out_width), BF16),
        input_output_aliases={4: 0},
        name="attn_meta",
        compiler_params=_params("arbitrary"),
    )(qkv, qkv, qkv, mb, att)


def kernel(x, meta_tokens, norm_mix_g, w_in, gate_bias, conv_w, conv_b, rel_bias, meta_bias,
           mlstm_norm_g, w_out, norm_ffn_g, w_gate, w_up, w_down, final_norm_g):
    batch, n_tok, d_model = x.shape
    assert batch == 1, "written for a single sequence"
    n_meta = meta_tokens.shape[0]
    depth = w_in.shape[0]
    t_rows = n_tok + n_meta
    att_heads = rel_bias.shape[1]
    n_gates = gate_bias.shape[1]
    ml_heads = n_gates // 4
    qk_width = conv_w.shape[2] // 2
    v_width = mlstm_norm_g.shape[1]
    att_width = (w_in.shape[2] - 2 * qk_width - 2 * v_width - n_gates) // 3
    hd_dim = att_width // att_heads
    dk = qk_width // ml_heads
    dv = v_width // ml_heads
    assert n_tok % (GRID_W * ROWS_PER_STEP) == 0 and n_meta % 16 == 0 and n_tok % n_meta == 0
    assert att_heads % HEADS_PER_STEP == 0

    tm = _divisor_tile(t_rows, 704, 16)
    tm_big = _divisor_tile(t_rows, 3328, 16)
    tr = tm
    conv_tile = _divisor_tile(n_tok, 512, CHUNK)

    seg_widths = (3 * att_width, 2 * qk_width, v_width, v_width)
    seg_dtypes = (BF16, F32, BF16, F32)
    q_scale = jnp.concatenate([jnp.full((qk_width,), dk ** -0.5, F32), jnp.ones((qk_width,), F32)])

    w_in_b = w_in.astype(BF16)
    w_out_b = w_out.astype(BF16)
    w_down_b = w_down.astype(BF16)
    attn_bias = _attention_bias(rel_bias)

    h = jnp.concatenate([x[0], meta_tokens.astype(x.dtype)], axis=0)
    for l in range(depth):
        u = _rmsnorm(h, norm_mix_g[l], t_rows, BF16, tr)
        qkv, mqk, mv, mo, g = _in_proj(u, w_in_b, gate_bias[l], l, seg_widths, seg_dtypes, tm)

        mix = _attention(qkv, attn_bias, l, meta_bias[l], n_tok, n_meta, att_heads, hd_dim,
                         att_width + v_width)
        qk = _qk_conv(mqk, conv_w[l], conv_b[l], q_scale, n_tok, n_meta, conv_tile)
        hf, hb = _mlstm_scan(qk, mv, g, n_tok, n_meta, ml_heads, dk, dv)
        mix = _mlstm_out(hf, hb, mo, mlstm_norm_g[l], mix, ml_heads, dv, tr)

        h = _matmul_residual(mix, w_out_b, l, h, tm, 1024, "out_proj")
        z = _rmsnorm(h, norm_ffn_g[l], t_rows, BF16, tr)
        act = _swiglu(z, w_gate, w_up, l, tm_big, 256)
        h = _matmul_residual(act, w_down_b, l, h, tm, 512, "down_proj")
    y = _rmsnorm(h, final_norm_g, n_tok, x.dtype, _divisor_tile(n_tok, 512, 16))
    return y[None]
```

```python
import functools

import jax
import jax.numpy as jnp
from jax import lax
from jax.experimental import pallas as pl
from jax.experimental.pallas import tpu as pltpu

F32 = jnp.float32
BF16 = jnp.bfloat16

GRID_W = 64
WIN_ROWS = 8
WIN_COLS = 16
CHUNK = 64
RMS_EPS = 1e-6
LANES = 128
LOG2E = 1.4426950408889634
NEG = -1e30
ROWS_PER_STEP = 32
HEADS_PER_STEP = 2

V7X_VMEM_BYTES = 64 * 1024 * 1024
V7X_VMEM_LIMIT = V7X_VMEM_BYTES - 8 * 1024 * 1024
V7X_VMEM_LIMIT_MAX = V7X_VMEM_BYTES - 3 * 1024 * 1024


def _params(*sem, vmem_limit=V7X_VMEM_LIMIT):
    return pltpu.CompilerParams(dimension_semantics=sem, vmem_limit_bytes=vmem_limit)


def _divisor_tile(n, target, mult):
    best = None
    for d in range(mult, min(n, target) + 1, mult):
        if n % d == 0:
            best = d
    return best if best is not None else n


def _sigmoid(x):
    return 1.0 / (1.0 + jnp.exp(-x))


def _rmsnorm_kernel(x_ref, g_ref, o_ref):
    x = x_ref[...]
    ms = jnp.mean(x * x, axis=-1, keepdims=True)
    o_ref[...] = (x * lax.rsqrt(ms + RMS_EPS) * g_ref[...]).astype(o_ref.dtype)


def _rmsnorm(x, g, out_rows, out_dtype, tile):
    d = x.shape[1]
    return pl.pallas_call(
        _rmsnorm_kernel,
        grid=(out_rows // tile,),
        in_specs=[pl.BlockSpec((tile, d), lambda i: (i, 0)),
                  pl.BlockSpec((1, d), lambda i: (0, 0))],
        out_specs=pl.BlockSpec((tile, d), lambda i: (i, 0)),
        out_shape=jax.ShapeDtypeStruct((out_rows, d), out_dtype),
        name="rmsnorm",
        compiler_params=_params("parallel"),
    )(x, g.reshape(1, d))


def _in_proj_kernel(u_ref, wt_ref, wgt_ref, bg_ref, *out_refs, bounds):
    j = pl.program_id(1)
    seg_refs, g_ref = out_refs[:-1], out_refs[-1]
    nt = (((1,), (1,)), ((), ()))
    for s, o_ref in enumerate(seg_refs):
        @pl.when(jnp.logical_and(j >= bounds[s], j < bounds[s + 1]))
        def _(o_ref=o_ref):
            o_ref[...] = lax.dot_general(u_ref[...], wt_ref[...], nt,
                                         preferred_element_type=F32).astype(o_ref.dtype)

    @pl.when(j == 0)
    def _():
        ng = g_ref.shape[1]
        g_ref[...] = lax.dot_general(u_ref[...], wgt_ref[:ng, :], nt,
                                     preferred_element_type=F32) + bg_ref[...]


def _in_proj(u, wt, gate_bias, layer, seg_widths, seg_dtypes, tm):
    m, k = u.shape
    ng = gate_bias.shape[0]
    gate_col0 = sum(seg_widths)
    assert gate_col0 % LANES == 0 and ng <= LANES and gate_col0 + ng == wt.shape[1]
    tn = next(t for t in (1024, 512, 256, 128) if all(sw % t == 0 for sw in seg_widths))
    bounds = [0]
    for sw in seg_widths:
        bounds.append(bounds[-1] + sw // tn)

    def seg_map(s):
        return lambda i, j: (i, jnp.clip(j - bounds[s], 0, bounds[s + 1] - bounds[s] - 1))

    kern = functools.partial(_in_proj_kernel, bounds=tuple(bounds))
    return pl.pallas_call(
        kern,
        grid=(m // tm, bounds[-1]),
        in_specs=[pl.BlockSpec((tm, k), lambda i, j: (i, 0)),
                  pl.BlockSpec((None, tn, k), lambda i, j: (layer, j, 0)),
                  pl.BlockSpec((None, LANES, k), lambda i, j: (layer, gate_col0 // LANES, 0)),
                  pl.BlockSpec((1, ng), lambda i, j: (0, 0))],
        out_specs=[pl.BlockSpec((tm, tn), seg_map(s)) for s in range(len(seg_widths))]
        + [pl.BlockSpec((tm, ng), lambda i, j: (i, 0))],
        out_shape=[jax.ShapeDtypeStruct((m, sw), dt) for sw, dt in zip(seg_widths, seg_dtypes)]
        + [jax.ShapeDtypeStruct((m, ng), F32)],
        name="in_proj",
        compiler_params=_params("parallel", "arbitrary"),
    )(u, wt, wt, gate_bias.reshape(1, ng))


def _mm_res_kernel(a_ref, w_ref, r_ref, o_ref):
    o_ref[...] = r_ref[...] + jnp.dot(a_ref[...], w_ref[...], preferred_element_type=F32)


def _matmul_residual(a, w, layer, res, tm, tn, name):
    m, k = a.shape
    n = w.shape[2]
    tn = min(tn, n)
    tile_bytes = 2 * (tm * k * a.dtype.itemsize + k * tn * w.dtype.itemsize + 2 * tm * tn * 4)
    vmem_limit = V7X_VMEM_LIMIT if tile_bytes < V7X_VMEM_LIMIT - (4 << 20) else V7X_VMEM_LIMIT_MAX
    return pl.pallas_call(
        _mm_res_kernel,
        grid=(m // tm, pl.cdiv(n, tn)),
        in_specs=[pl.BlockSpec((tm, k), lambda i, j: (i, 0)),
                  pl.BlockSpec((None, k, tn), lambda i, j: (layer, 0, j)),
                  pl.BlockSpec((tm, tn), lambda i, j: (i, j))],
        out_specs=pl.BlockSpec((tm, tn), lambda i, j: (i, j)),
        out_shape=jax.ShapeDtypeStruct((m, n), F32),
        name=name,
        compiler_params=_params("parallel", "arbitrary", vmem_limit=vmem_limit),
    )(a, w, res)


def _swiglu_kernel(a_ref, wg_ref, wu_ref, o_ref):
    a = a_ref[...]
    g = jnp.dot(a, wg_ref[...].astype(a.dtype), preferred_element_type=F32)
    u = jnp.dot(a, wu_ref[...].astype(a.dtype), preferred_element_type=F32)
    o_ref[...] = (g * _sigmoid(g) * u).astype(o_ref.dtype)


def _swiglu(a, wg, wu, layer, tm, tn):
    m, k = a.shape
    n = wg.shape[2]
    tn = min(tn, n)
    return pl.pallas_call(
        _swiglu_kernel,
        grid=(m // tm, pl.cdiv(n, tn)),
        in_specs=[pl.BlockSpec((tm, k), lambda i, j: (i, 0), pipeline_mode=pl.Buffered(1)),
                  pl.BlockSpec((None, k, tn), lambda i, j: (layer, 0, j)),
                  pl.BlockSpec((None, k, tn), lambda i, j: (layer, 0, j))],
        out_specs=pl.BlockSpec((tm, tn), lambda i, j: (i, j)),
        out_shape=jax.ShapeDtypeStruct((m, n), BF16),
        name="swiglu",
        compiler_params=_params("parallel", "arbitrary"),
    )(a, wg, wu)


def _conv_kernel(x_ref, prev_ref, next_ref, w_ref, b_ref, s_ref, o_ref, ext_ref, *, n_meta, half):
    i = pl.program_id(0)
    n = pl.num_programs(0)
    tile = x_ref.shape[0]
    is_meta = i == n - 1
    row = lax.broadcasted_iota(jnp.int32, x_ref.shape, 0)
    x = jnp.where(jnp.logical_and(is_meta, row >= n_meta), 0.0, x_ref[...])
    ext_ref[0:8, :] = jnp.where(is_meta, 0.0, prev_ref[...])
    ext_ref[8:8 + tile, :] = x
    ext_ref[8 + tile:16 + tile, :] = jnp.where(i == n - 2, 0.0, next_ref[...])

    width = 2 * half + 1

    def conv_rows(start, rows):
        y = b_ref[...] + w_ref[0:1, :] * ext_ref[pl.ds(8 + start - half, rows), :]
        for j in range(1, width):
            y = y + w_ref[j:j + 1, :] * ext_ref[pl.ds(8 + start - half + j, rows), :]
        return y

    def finish(y):
        return (y * _sigmoid(y) * s_ref[...]).astype(o_ref.dtype)

    out = finish(conv_rows(0, tile))
    o_ref[...] = jnp.where(jnp.logical_and(is_meta, row >= n_meta), jnp.zeros_like(out), out)

    @pl.when(is_meta)
    def _():
        base = n_meta - 8
        y = conv_rows(base, 8)
        r8 = lax.broadcasted_iota(jnp.int32, y.shape, 0) + base
        for t in range(half):
            for j in range(half + 1 + t, width):
                src = j - half - 1 - t
                y = y + jnp.where(r8 == n_meta - 1 - t,
                                  w_ref[j:j + 1, :] * next_ref[src:src + 1, :], 0.0)
        o_ref[base:base + 8, :] = finish(y)


def _qk_conv(x, w, b, scale, n_tok, n_meta, tile):
    c = x.shape[1]
    n_grid_tiles = n_tok // tile
    kw = w.shape[0]
    half = kw // 2
    t8 = tile // 8
    last8 = (n_tok + n_meta) // 8 - 1

    def prev_map(i):
        return (jnp.where(i == 0, last8, jnp.minimum(i, n_grid_tiles) * t8 - 1), 0)

    def next_map(i):
        return (jnp.where(i >= n_grid_tiles - 1, 0, (i + 1) * t8), 0)

    kern = functools.partial(_conv_kernel, n_meta=n_meta, half=half)
    return pl.pallas_call(
        kern,
        grid=(n_grid_tiles + 1,),
        in_specs=[pl.BlockSpec((tile, c), lambda i: (i, 0)),
                  pl.BlockSpec((8, c), prev_map),
                  pl.BlockSpec((8, c), next_map),
                  pl.BlockSpec((kw, c), lambda i: (0, 0)),
                  pl.BlockSpec((1, c), lambda i: (0, 0)),
                  pl.BlockSpec((1, c), lambda i: (0, 0))],
        out_specs=pl.BlockSpec((tile, c), lambda i: (i, 0)),
        out_shape=jax.ShapeDtypeStruct((n_tok + CHUNK, c), BF16),
        scratch_shapes=[pltpu.VMEM((tile + 16, c), F32)],
        name="mlstm_qk_conv",
        compiler_params=_params("parallel"),
    )(x, x, x, w, b.reshape(1, c), scale.reshape(1, c))


def _split_cumsum(mask, x, mask_first):
    m = mask.astype(BF16)
    total = None
    rest = x
    for _ in range(3):
        part = rest.astype(BF16)
        rest = rest - part.astype(F32)
        term = (jnp.dot(m, part, preferred_element_type=F32) if mask_first
                else jnp.dot(part, m, preferred_element_type=F32))
        total = term if total is None else total + term
    return total


def _scan_kernel(qkf_ref, qkb_ref, vf_ref, vb_ref, gf_ref, gb_ref,
                 hf_ref, hb_ref, c_ref, n_ref, m_ref, *, n_heads, dk, dv, n_meta):
    j = pl.program_id(0)
    nsteps = pl.num_programs(0)
    L = CHUNK

    @pl.when(j == 0)
    def _():
        c_ref[...] = jnp.zeros_like(c_ref)
        n_ref[...] = jnp.zeros_like(n_ref)
        m_ref[...] = jnp.zeros_like(m_ref)

    t_idx = lax.broadcasted_iota(jnp.int32, (L, L), 0)
    s_idx = lax.broadcasted_iota(jnp.int32, (L, L), 1)
    lower = t_idx >= s_idx
    upper = t_idx <= s_idx
    row_l = lax.broadcasted_iota(jnp.int32, (L, 1), 0)

    gate_col = lax.broadcasted_iota(jnp.int32, (L, 4 * n_heads), 1)
    is_forget = (gate_col // n_heads) % 2 == 1

    dirs = (
        (qkf_ref, vf_ref, gf_ref, hf_ref, lower, lower, upper, j == 0),
        (qkb_ref, vb_ref, gb_ref, hb_ref, upper, upper, lower, j == nsteps - 1),
    )
    chains = []
    for d, (qk_ref, v_ref, g_ref, h_ref, mask, cum_col, cum_row, is_meta) in enumerate(dirs):
        valid = jnp.logical_or(jnp.logical_not(is_meta), row_l < n_meta)
        pre = g_ref[...]
        log_sig = jnp.minimum(pre, 0.0) - jnp.log(1.0 + jnp.exp(-jnp.abs(pre)))
        g = jnp.where(valid, jnp.where(is_forget, log_sig, pre), jnp.where(is_forget, 0.0, NEG))
        gt = g.T
        c0 = 2 * d * n_heads
        ig_cols = g[:, c0:c0 + n_heads]
        lf_cols = g[:, c0 + n_heads:c0 + 2 * n_heads]
        ig_rows = gt[c0:c0 + n_heads, :]
        lf_rows = gt[c0 + n_heads:c0 + 2 * n_heads, :]
        b_cols = _split_cumsum(cum_col, lf_cols, True)
        b_rows = _split_cumsum(cum_row, lf_rows, False)
        b_tots = jnp.sum(lf_rows, axis=1, keepdims=True)

        for hd in range(n_heads):
            ch = d * n_heads + hd
            q = qk_ref[:, hd * dk:(hd + 1) * dk]
            k = qk_ref[:, (n_heads + hd) * dk:(n_heads + hd + 1) * dk]
            v = v_ref[:, hd * dv:(hd + 1) * dv]
            v = jnp.where(valid, v, jnp.zeros_like(v))
            b_col = b_cols[:, hd:hd + 1]
            b_row = b_rows[hd:hd + 1, :]
            ig_col = ig_cols[:, hd:hd + 1]
            ig_row = ig_rows[hd:hd + 1, :]
            b_tot = b_tots[hd:hd + 1, :]
            m = m_ref[ch]

            log_d = jnp.where(mask, b_col - b_row + ig_row, NEG)
            log_inter = b_col + m
            m_t = jnp.maximum(log_inter, jnp.max(log_d, axis=1, keepdims=True))
            log_w_row = b_tot - b_row + ig_row
            log_w_col = b_tot - b_col + ig_col
            m_new = jnp.maximum(b_tot + m, jnp.max(log_w_row, axis=1, keepdims=True))
            chains.append(dict(
                ch=ch, q=q, k=k, v=v, m_t=m_t, m_new=m_new, h_ref=h_ref,
                cols=slice(hd * dv, (hd + 1) * dv),
                d_mat=jnp.exp(log_d - m_t),
                inter=jnp.exp(log_inter - m_t),
                decay=jnp.exp(b_tot + m - m_new),
                w_col=jnp.exp(log_w_col - m_new)))

    for c in chains:
        c["qk"] = lax.dot_general(c["q"], c["k"], (((1,), (1,)), ((), ())),
                                  preferred_element_type=F32)
        c["qc"] = jnp.dot(c["q"], c_ref[c["ch"]].astype(BF16), preferred_element_type=F32)
    for c in chains:
        c["s"] = c["qk"] * c["d_mat"]
        c["wv"] = (c["w_col"] * c["v"].astype(F32)).astype(BF16)
        qn = jnp.sum(c["q"].astype(F32) * n_ref[c["ch"]], axis=1, keepdims=True)
        den = c["inter"] * qn + jnp.sum(c["s"], axis=1, keepdims=True)
        c["scale"] = 1.0 / jnp.maximum(jnp.abs(den), jnp.exp(-c["m_t"]))
    for c in chains:
        ch, decay = c["ch"], c["decay"]
        sv = jnp.dot(c["s"].astype(BF16), c["v"], preferred_element_type=F32)
        c["h_ref"][:, c["cols"]] = (c["inter"] * c["qc"] + sv) * c["scale"]
        d_c = lax.dot_general(c["k"], c["wv"], (((0,), (0,)), ((), ())),
                              preferred_element_type=F32)
        c_ref[ch] = decay * c_ref[ch] + d_c
        n_ref[ch] = decay * n_ref[ch] + jnp.sum(c["w_col"] * c["k"].astype(F32), axis=0,
                                                keepdims=True)
        m_ref[ch] = c["m_new"]


def _mlstm_scan(qk, v, g, n_tok, n_meta, n_heads, dk, dv):
    t_rows = v.shape[0]
    n_chunks = n_tok // CHUNK + 1
    ng = g.shape[1]

    def fwd(j):
        return (j + n_chunks - 1) % n_chunks

    def bwd(j):
        return (2 * (n_chunks - 1) - j) % n_chunks

    kern = functools.partial(_scan_kernel, n_heads=n_heads, dk=dk, dv=dv, n_meta=n_meta)
    return pl.pallas_call(
        kern,
        grid=(n_chunks,),
        in_specs=[pl.BlockSpec((CHUNK, qk.shape[1]), lambda j: (fwd(j), 0)),
                  pl.BlockSpec((CHUNK, qk.shape[1]), lambda j: (bwd(j), 0)),
                  pl.BlockSpec((CHUNK, v.shape[1]), lambda j: (fwd(j), 0)),
                  pl.BlockSpec((CHUNK, v.shape[1]), lambda j: (bwd(j), 0)),
                  pl.BlockSpec((CHUNK, ng), lambda j: (fwd(j), 0)),
                  pl.BlockSpec((CHUNK, ng), lambda j: (bwd(j), 0))],
        out_specs=[pl.BlockSpec((CHUNK, v.shape[1]), lambda j: (fwd(j), 0)),
                   pl.BlockSpec((CHUNK, v.shape[1]), lambda j: (bwd(j), 0))],
        out_shape=[jax.ShapeDtypeStruct((t_rows, v.shape[1]), F32),
                   jax.ShapeDtypeStruct((t_rows, v.shape[1]), F32)],
        scratch_shapes=[pltpu.VMEM((2 * n_heads, dk, dv), F32),
                        pltpu.VMEM((2 * n_heads, 1, dk), F32),
                        pltpu.VMEM((2 * n_heads, 1, 1), F32)],
        name="mlstm_scan",
        compiler_params=_params("arbitrary"),
    )(qk, qk, v, v, g, g)


def _mlstm_out_kernel(hf_ref, hb_ref, o_ref, g_ref, mix_in_ref, out_ref, *, n_heads, dv):
    del mix_in_ref
    for hd in range(n_heads):
        sl = slice(hd * dv, (hd + 1) * dv)
        h = hf_ref[:, sl] + hb_ref[:, sl]
        ms = jnp.mean(h * h, axis=-1, keepdims=True)
        h = h * lax.rsqrt(ms + RMS_EPS)
        out_ref[:, sl] = (h * g_ref[:, sl] * _sigmoid(o_ref[:, sl])).astype(out_ref.dtype)


def _mlstm_out(hf, hb, o_pre, norm_g, mix, n_heads, dv, tile):
    t_rows, width = hf.shape
    col_blk = (mix.shape[1] - width) // width
    assert col_blk * width == mix.shape[1] - width
    kern = functools.partial(_mlstm_out_kernel, n_heads=n_heads, dv=dv)
    spec = pl.BlockSpec((tile, width), lambda i: (i, 0))
    return pl.pallas_call(
        kern,
        grid=(t_rows // tile,),
        in_specs=[spec, spec, spec, pl.BlockSpec((1, width), lambda i: (0, 0)),
                  pl.BlockSpec(memory_space=pl.ANY)],
        out_specs=pl.BlockSpec((tile, width), lambda i: (i, col_blk)),
        out_shape=jax.ShapeDtypeStruct(mix.shape, mix.dtype),
        input_output_aliases={4: 0},
        name="mlstm_out",
        compiler_params=_params("parallel"),
    )(hf, hb, o_pre, norm_g.reshape(1, width), mix)


def _attn_kernel(q_ref, k_ref, v_ref, km_ref, vm_ref, bias_ref, mb_ref, o_ref, *, n_rows, hd_dim):
    blk = pl.program_id(1)
    scale = hd_dim ** -0.5 * LOG2E
    nt = (((1,), (1,)), ((), ()))

    def meta_part(hh):
        cols = slice(hh * hd_dim, (hh + 1) * hd_dim)
        sm = lax.dot_general(q_ref[:, cols], km_ref[:, cols], nt,
                             preferred_element_type=F32) * scale + mb_ref[hh]
        m_meta = jnp.max(sm, axis=1, keepdims=True)
        pm = jnp.exp2(sm - m_meta)
        l_meta = jnp.sum(pm, axis=1, keepdims=True)
        o_meta = jnp.dot(pm.astype(BF16), vm_ref[:, cols], preferred_element_type=F32)
        return m_meta, l_meta, o_meta

    units = [(rr, hh) for rr in range(ROWS_PER_STEP) for hh in range(HEADS_PER_STEP)]

    def window_start(rr):
        r = blk * ROWS_PER_STEP + rr
        rs = jnp.clip(r - WIN_ROWS // 2, 0, n_rows - WIN_ROWS)
        return pl.multiple_of(rs * GRID_W, GRID_W), rs - r + WIN_ROWS - 1

    def scores(rr, hh):
        cols = slice(hh * hd_dim, (hh + 1) * hd_dim)
        k0, variant = window_start(rr)
        q = q_ref[rr * GRID_W:(rr + 1) * GRID_W, cols]
        kw = k_ref[pl.ds(k0, WIN_ROWS * GRID_W), cols]
        return lax.dot_general(q, kw, nt, preferred_element_type=F32) * scale + bias_ref[variant, hh]

    def finish(rr, hh, s):
        cols = slice(hh * hd_dim, (hh + 1) * hd_dim)
        rows = slice(rr * GRID_W, (rr + 1) * GRID_W)
        k0, _ = window_start(rr)
        m_meta, l_meta, o_meta = meta[hh]
        m_loc = jnp.max(s, axis=1, keepdims=True)
        p = jnp.exp2(s - m_loc)
        l_loc = jnp.sum(p, axis=1, keepdims=True)
        vw = v_ref[pl.ds(k0, WIN_ROWS * GRID_W), cols]
        o_loc = jnp.dot(p.astype(BF16), vw, preferred_element_type=F32)
        m_all = jnp.maximum(m_loc, m_meta[rows])
        a_loc = jnp.exp2(m_loc - m_all)
        a_meta = jnp.exp2(m_meta[rows] - m_all)
        o = (a_loc * o_loc + a_meta * o_meta[rows]) / (a_loc * l_loc + a_meta * l_meta[rows])
        o_ref[rows, cols] = o.astype(o_ref.dtype)

    ahead = 4
    pending = [scores(*units[u]) for u in range(ahead)]
    meta = [meta_part(hh) for hh in range(HEADS_PER_STEP)]
    for u, (rr, hh) in enumerate(units):
        if u + ahead < len(units):
            pending.append(scores(*units[u + ahead]))
        finish(rr, hh, pending.pop(0))


def _attn_meta_kernel(q_ref, k_ref, v_ref, mb_ref, att_in_ref, o_ref, *, n_heads, hd_dim):
    del att_in_ref
    scale = hd_dim ** -0.5
    for h in range(n_heads):
        cols = slice(h * hd_dim, (h + 1) * hd_dim)
        s = lax.dot_general(q_ref[:, cols], k_ref[:, cols], (((1,), (1,)), ((), ())),
                            preferred_element_type=F32) * scale + mb_ref[h]
        p = jnp.exp(s - jnp.max(s, axis=1, keepdims=True))
        denom = jnp.sum(p, axis=1, keepdims=True)
        o = jnp.dot(p.astype(BF16), v_ref[:, cols], preferred_element_type=F32)
        o_ref[:, cols] = (o / denom).astype(o_ref.dtype)


def _attention_bias(rel_bias):
    depth, n_heads = rel_bias.shape[:2]
    c = jnp.arange(GRID_W)
    cs = jnp.clip(c - WIN_COLS // 2, 0, GRID_W - WIN_COLS)
    in_win = (c[None, :] >= cs[:, None]) & (c[None, :] < cs[:, None] + WIN_COLS)
    rb = jnp.pad(rel_bias.astype(F32) * LOG2E, ((0, 0),) * 3 + ((GRID_W, GRID_W),))
    e = jnp.stack([rb[..., GRID_W + WIN_COLS - 1 - q:2 * GRID_W + WIN_COLS - 1 - q]
                   for q in range(GRID_W)], axis=3)
    e = jnp.where(in_win, e, NEG)
    bv = jnp.stack([e[:, :, v:v + WIN_ROWS] for v in range(WIN_ROWS)], axis=1)
    bv = bv.transpose(0, 1, 2, 4, 3, 5)
    return bv.reshape(depth, WIN_ROWS, n_heads, GRID_W, WIN_ROWS * GRID_W)


def _attention(qkv, bias, layer, meta_bias, n_tok, n_meta, n_heads, hd_dim, out_width):
    t_rows = qkv.shape[0]
    n_rows = n_tok // GRID_W
    width = n_heads * hd_dim
    gw = HEADS_PER_STEP * hd_dim
    n_groups = n_heads // HEADS_PER_STEP
    q_tile = ROWS_PER_STEP * GRID_W
    mb = meta_bias.astype(F32).reshape(n_heads, 1, n_meta)
    meta_blk = n_tok // n_meta

    kern = functools.partial(_attn_kernel, n_rows=n_rows, hd_dim=hd_dim)
    att = pl.pallas_call(
        kern,
        grid=(n_groups, n_rows // ROWS_PER_STEP),
        in_specs=[pl.BlockSpec((q_tile, gw), lambda g, b: (b, g)),
                  pl.BlockSpec((n_tok, gw), lambda g, b: (0, n_groups + g)),
                  pl.BlockSpec((n_tok, gw), lambda g, b: (0, 2 * n_groups + g)),
                  pl.BlockSpec((n_meta, gw), lambda g, b: (meta_blk, n_groups + g)),
                  pl.BlockSpec((n_meta, gw), lambda g, b: (meta_blk, 2 * n_groups + g)),
                  pl.BlockSpec((None, WIN_ROWS, HEADS_PER_STEP, GRID_W, WIN_ROWS * GRID_W),
                               lambda g, b: (layer, 0, g, 0, 0)),
                  pl.BlockSpec((HEADS_PER_STEP, 1, n_meta), lambda g, b: (g, 0, 0))],
        out_specs=pl.BlockSpec((q_tile, gw), lambda g, b: (b, g)),
        out_shape=jax.ShapeDtypeStruct((t_rows, out_width), BF16),
        name="attn",
        compiler_params=_params("parallel", "arbitrary"),
    )(qkv, qkv, qkv, qkv, qkv, bias, mb * LOG2E)

    meta_kern = functools.partial(_attn_meta_kernel, n_heads=n_heads, hd_dim=hd_dim)
    return pl.pallas_call(
        meta_kern,
        grid=(1,),
        in_specs=[pl.BlockSpec((n_meta, width), lambda i: (meta_blk, 0)),
                  pl.BlockSpec((n_meta, width), lambda i: (meta_blk, 1)),
                  pl.BlockSpec((n_meta, width), lambda i: (meta_blk, 2)),
                  pl.BlockSpec((n_heads, 1, n_meta), lambda i: (0, 0, 0)),
                  pl.BlockSpec(memory_space=pl.ANY)],
        out_specs=pl.BlockSpec((n_meta, width), lambda i: (meta_blk, 0)),
        out_shape=jax.ShapeDtypeStruct((t_rows, out_width), BF16),
        input_output_aliases={4: 0},
        name="attn_meta",
        compiler_params=_params("arbitrary"),
    )(qkv, qkv, qkv, mb, att)


def kernel(x, meta_tokens, norm_mix_g, w_in, gate_bias, conv_w, conv_b, rel_bias, meta_bias,
           mlstm_norm_g, w_out, norm_ffn_g, w_gate, w_up, w_down, final_norm_g):
    batch, n_tok, d_model = x.shape
    assert batch == 1, "written for a single sequence"
    n_meta = meta_tokens.shape[0]
    depth = w_in.shape[0]
    t_rows = n_tok + n_meta
    att_heads = rel_bias.shape[1]
    n_gates = gate_bias.shape[1]
    ml_heads = n_gates // 4
    qk_width = conv_w.shape[2] // 2
    v_width = mlstm_norm_g.shape[1]
    att_width = (w_in.shape[2] - 2 * qk_width - 2 * v_width - n_gates) // 3
    hd_dim = att_width // att_heads
    dk = qk_width // ml_heads
    dv = v_width // ml_heads
    assert n_tok % (GRID_W * ROWS_PER_STEP) == 0 and n_meta % 16 == 0 and n_tok % n_meta == 0
    assert att_heads % HEADS_PER_STEP == 0

    tm = _divisor_tile(t_rows, 704, 16)
    tm_big = _divisor_tile(t_rows, 3328, 16)
    tr = tm
    conv_tile = _divisor_tile(n_tok, 512, CHUNK)

    seg_widths = (3 * att_width, 2 * qk_width, v_width, v_width)
    seg_dtypes = (BF16, F32, BF16, F32)
    q_scale = jnp.concatenate([jnp.full((qk_width,), dk ** -0.5, F32), jnp.ones((qk_width,), F32)])

    w_in_t = jnp.swapaxes(w_in, 1, 2).astype(BF16)
    w_out_b = w_out.astype(BF16)
    w_down_b = w_down.astype(BF16)
    attn_bias = _attention_bias(rel_bias)

    h = jnp.concatenate([x[0], meta_tokens.astype(x.dtype)], axis=0)
    for l in range(depth):
        u = _rmsnorm(h, norm_mix_g[l], t_rows, BF16, tr)
        qkv, mqk, mv, mo, g = _in_proj(u, w_in_t, gate_bias[l], l, seg_widths, seg_dtypes, tm)

        mix = _attention(qkv, attn_bias, l, meta_bias[l], n_tok, n_meta, att_heads, hd_dim,
                         att_width + v_width)
        qk = _qk_conv(mqk, conv_w[l], conv_b[l], q_scale, n_tok, n_meta, conv_tile)
        hf, hb = _mlstm_scan(qk, mv, g, n_tok, n_meta, ml_heads, dk, dv)
        mix = _mlstm_out(hf, hb, mo, mlstm_norm_g[l], mix, ml_heads, dv, tr)

        h = _matmul_residual(mix, w_out_b, l, h, tm, 1024, "out_proj")
        z = _rmsnorm(h, norm_ffn_g[l], t_rows, BF16, tr)
        act = _swiglu(z, w_gate, w_up, l, tm_big, 256)
        h = _matmul_residual(act, w_down_b, l, h, tm, 512, "down_proj")
    y = _rmsnorm(h, final_norm_g, n_tok, x.dtype, _divisor_tile(n_tok, 512, 16))
    return y[None]
```
